```python
import math
import jax
import jax.numpy as jnp
from jax import lax
import numpy as np

D_MODEL = 1024
BATCH = 16
SEQ = 2048
DEPTH = 4
DEC_BATCH = 128
DEC_SEQ = 4
PAST_LEN = 8192
PAGE_SIZE = 128

N_MIXERS = 3
N_MLA_LAYERS = (DEPTH + 2) // 3
N_SB_LAYERS = (DEPTH + 1) // 3
N_GDN_LAYERS = DEPTH // 3

D_FF = 2816
NORM_EPS = 1e-6
Q_BLOCK = 128

MLA_HEADS = 16
MLA_Q_LORA = 256
MLA_KV_LORA = 256
MLA_D_NOPE = 128
MLA_D_ROPE = 64
MLA_D_V = 128
MLA_IN_DIM = MLA_Q_LORA + MLA_KV_LORA + MLA_D_ROPE
ROPE_THETA = 10000.0

SB_HEADS = 16
SB_HEAD_DIM = D_MODEL // SB_HEADS
SB_DIM = SB_HEADS * SB_HEAD_DIM

GDN_HEADS = 8
GDN_DK = 128
GDN_DV = 128
GDN_KEY_DIM = GDN_HEADS * GDN_DK
GDN_VAL_DIM = GDN_HEADS * GDN_DV
GDN_CONV_DIM = 2 * GDN_KEY_DIM + GDN_VAL_DIM
GDN_CONV_W = 4
GDN_CHUNK = 64
GDN_IN_DIM = GDN_CONV_DIM + GDN_VAL_DIM + 2 * GDN_HEADS

kernel_name = 'hybrid_mla_stickbreak_gdn_macaron_step'


def rmsnorm(x, w):
    xf = x.astype(jnp.float32)
    y = xf * lax.rsqrt(jnp.mean(xf * xf, axis=-1, keepdims=True) + NORM_EPS)
    return (y * w.astype(jnp.float32)).astype(x.dtype)


def l2norm(x):
    xf = x.astype(jnp.float32)
    return (xf * lax.rsqrt(jnp.sum(xf * xf, axis=-1, keepdims=True) + NORM_EPS)).astype(x.dtype)


def half_ffn(x, g, w_gate, w_up, w_down):
    h = rmsnorm(x, g)
    return x + 0.5 * ((jax.nn.silu(h @ w_gate) * (h @ w_up)) @ w_down)


def rope(x, pos):
    half = x.shape[-1] // 2
    inv = ROPE_THETA ** (-jnp.arange(half, dtype=jnp.float32) / half)
    ang = pos.astype(jnp.float32)[:, None] * inv[None, :]
    shape = (ang.shape[0],) + (1,) * (x.ndim - 3) + (half,)
    cos = jnp.cos(ang).reshape(shape)
    sin = jnp.sin(ang).reshape(shape)
    xf = x.astype(jnp.float32)
    x1, x2 = xf[..., :half], xf[..., half:]
    return jnp.concatenate([x1 * cos - x2 * sin, x2 * cos + x1 * sin], axis=-1).astype(x.dtype)


def to_blocks(a):
    b, t = a.shape[:2]
    return jnp.moveaxis(a.reshape((b, t // Q_BLOCK, Q_BLOCK) + a.shape[2:]), 1, 0)


def from_blocks(a):
    a = jnp.moveaxis(a, 0, 1)
    return a.reshape((a.shape[0], a.shape[1] * a.shape[2]) + a.shape[3:])


def gather_pages(cache, page_table):
    g = cache[page_table]
    return g.reshape((g.shape[0], g.shape[1] * g.shape[2]) + g.shape[3:])


def mla_project(h, pos, w_in, norm_q, norm_kv, w_uq, w_uk):
    proj = h @ w_in
    c_q = rmsnorm(proj[..., :MLA_Q_LORA], norm_q)
    c_kv = rmsnorm(proj[..., MLA_Q_LORA:MLA_Q_LORA + MLA_KV_LORA], norm_kv)
    k_rope = rope(proj[..., MLA_Q_LORA + MLA_KV_LORA:], pos)
    q = jnp.einsum('btl,lhe->bthe', c_q, w_uq)
    q_rope = rope(q[..., MLA_D_NOPE:], pos)
    q_lat = jnp.einsum('bthn,chn->bthc', q[..., :MLA_D_NOPE], w_uk)
    return q_lat, q_rope, c_kv, k_rope


def mla_attend(q_lat, q_rope, c_kv, k_rope, q_pos, k_pos):
    s = (jnp.einsum('bqhc,bkc->bhqk', q_lat, c_kv)
         + jnp.einsum('bqhr,bkr->bhqk', q_rope, k_rope)).astype(jnp.float32)
    s = s * ((MLA_D_NOPE + MLA_D_ROPE) ** -0.5)
    causal = k_pos[None, :] <= q_pos[:, None]
    s = jnp.where(causal, s, jnp.finfo(jnp.float32).min)
    p = jax.nn.softmax(s, axis=-1).astype(c_kv.dtype)
    return jnp.einsum('bhqk,bkc->bqhc', p, c_kv)


def mla_out(o_lat, w_uv, w_o):
    o = jnp.einsum('bqhc,chv->bqhv', o_lat, w_uv)
    return o.reshape(o.shape[0], o.shape[1], MLA_HEADS * MLA_D_V) @ w_o


def mla_layer(hp, hs, ckv_cache, krope_cache, page_table, w_in, norm_q, norm_kv, w_uq, w_uk, w_uv, w_o):
    pos_p = jnp.arange(hp.shape[1], dtype=jnp.int32)
    ql_p, qr_p, ckv_p, kr_p = mla_project(hp, pos_p, w_in, norm_q, norm_kv, w_uq, w_uk)
    ol_p = from_blocks(lax.map(
        lambda blk: mla_attend(blk[0], blk[1], ckv_p, kr_p, blk[2], pos_p),
        (to_blocks(ql_p), to_blocks(qr_p), pos_p.reshape(-1, Q_BLOCK))))
    past = page_table.shape[1] * ckv_cache.shape[1]
    pos_s = past + jnp.arange(hs.shape[1], dtype=jnp.int32)
    ql_s, qr_s, ckv_s, kr_s = mla_project(hs, pos_s, w_in, norm_q, norm_kv, w_uq, w_uk)
    ckv_all = jnp.concatenate([gather_pages(ckv_cache, page_table), ckv_s], axis=1)
    kr_all = jnp.concatenate([gather_pages(krope_cache, page_table), kr_s], axis=1)
    k_pos = jnp.arange(past + hs.shape[1], dtype=jnp.int32)
    ol_s = mla_attend(ql_s, qr_s, ckv_all, kr_all, pos_s, k_pos)
    return mla_out(ol_p, w_uv, w_o), mla_out(ol_s, w_uv, w_o), ckv_p, kr_p, ckv_s, kr_s


def sb_attend(q, k, v, q_pos, k_pos):
    z = jnp.einsum('bqhd,bkhd->bhqk', q, k).astype(jnp.float32) * (SB_HEAD_DIM ** -0.5)
    causal = k_pos[None, :] < q_pos[:, None]
    log_keep = jnp.where(causal, jax.nn.log_sigmoid(-z), 0.0)
    later = lax.cumsum(log_keep, axis=3, reverse=True) - log_keep
    a = jnp.where(causal, jnp.exp(jax.nn.log_sigmoid(z) + later), 0.0)
    return jnp.einsum('bhqk,bkhd->bqhd', a.astype(v.dtype), v)


def sb_project(h, w_in):
    b, t, _ = h.shape
    proj = h @ w_in
    q = proj[..., :SB_DIM].reshape(b, t, SB_HEADS, SB_HEAD_DIM)
    k = proj[..., SB_DIM:2 * SB_DIM].reshape(b, t, SB_HEADS, SB_HEAD_DIM)
    v = proj[..., 2 * SB_DIM:].reshape(b, t, SB_HEADS, SB_HEAD_DIM)
    return q, k, v


def sb_layer(hp, hs, k_cache, v_cache, page_table, w_in, w_o):
    pos_p = jnp.arange(hp.shape[1], dtype=jnp.int32)
    q_p, k_p, v_p = sb_project(hp, w_in)
    o_p = from_blocks(lax.map(
        lambda blk: sb_attend(blk[0], k_p, v_p, blk[1], pos_p),
        (to_blocks(q_p), pos_p.reshape(-1, Q_BLOCK))))
    past = page_table.shape[1] * k_cache.shape[1]
    pos_s = past + jnp.arange(hs.shape[1], dtype=jnp.int32)
    q_s, k_s, v_s = sb_project(hs, w_in)
    k_all = jnp.concatenate([gather_pages(k_cache, page_table), k_s], axis=1)
    v_all = jnp.concatenate([gather_pages(v_cache, page_table), v_s], axis=1)
    k_pos = jnp.arange(past + hs.shape[1], dtype=jnp.int32)
    o_s = sb_attend(q_s, k_all, v_all, pos_s, k_pos)
    y_p = o_p.reshape(o_p.shape[0], o_p.shape[1], SB_DIM) @ w_o
    y_s = o_s.reshape(o_s.shape[0], o_s.shape[1], SB_DIM) @ w_o
    return y_p, y_s, k_p, v_p, k_s, v_s


def causal_conv(x, buf, w):
    t = x.shape[1]
    xp = jnp.concatenate([buf, x], axis=1)
    y = xp[:, 0:t] * w[0]
    for i in range(1, GDN_CONV_W):
        y = y + xp[:, i:i + t] * w[i]
    return y, xp[:, xp.shape[1] - (GDN_CONV_W - 1):]


def gated_delta_rule(q, k, v, g, beta, s0):
    out_dtype = v.dtype
    b, t, h, dk = q.shape
    c = GDN_CHUNK if t % GDN_CHUNK == 0 else t
    n = t // c

    def chunks(a):
        a = a.astype(jnp.float32).reshape((b, n, c) + a.shape[2:])
        return jnp.moveaxis(a, (1, 3), (0, 2))

    qc = chunks(q) * (dk ** -0.5)
    kc, vc, gc, bc = chunks(k), chunks(v), chunks(g), chunks(beta)
    g_cum = jnp.cumsum(gc, axis=-1)
    idx = jnp.arange(c)
    incl = idx[:, None] >= idx[None, :]
    strict = idx[:, None] > idx[None, :]
    decay = jnp.where(incl, jnp.exp(jnp.where(incl, g_cum[..., :, None] - g_cum[..., None, :], 0.0)), 0.0)
    kb = kc * bc[..., None]
    vb = vc * bc[..., None]
    m = jnp.where(strict, jnp.einsum('...id,...jd->...ij', kb, kc) * decay, 0.0)
    a_mat = m + jnp.eye(c, dtype=jnp.float32)
    u = lax.linalg.triangular_solve(a_mat, vb, left_side=True, lower=True, unit_diagonal=True)
    w = lax.linalg.triangular_solve(a_mat, kb * jnp.exp(g_cum)[..., None], left_side=True, lower=True,
                                    unit_diagonal=True)
    qk = jnp.where(incl, jnp.einsum('...id,...jd->...ij', qc, kc) * decay, 0.0)

    def step(s, inp):
        qi, ki, ui, wi, gi, qki = inp
        v_new = ui - jnp.einsum('bhcd,bhde->bhce', wi, s)
        o = (jnp.einsum('bhcd,bhde->bhce', qi * jnp.exp(gi)[..., None], s)
             + jnp.einsum('bhij,bhje->bhie', qki, v_new))
        g_last = gi[..., -1:]
        s = (s * jnp.exp(g_last)[..., None]
             + jnp.einsum('bhcd,bhce->bhde', ki * jnp.exp(g_last - gi)[..., None], v_new))
        return s, o

    s_fin, o = lax.scan(step, s0.astype(jnp.float32), (qc, kc, u, w, g_cum, qk))
    o = jnp.moveaxis(o, (0, 2), (1, 3)).reshape(b, t, h, GDN_DV)
    return o.astype(out_dtype), s_fin.astype(s0.dtype)


def gdn_mixer(h, conv_buf, s0, w_in, conv_w, a_log, dt_bias, norm_w, w_o):
    b, t, _ = h.shape
    proj = h @ w_in
    qkv = proj[..., :GDN_CONV_DIM]
    z = proj[..., GDN_CONV_DIM:GDN_CONV_DIM + GDN_VAL_DIM]
    b_raw = proj[..., GDN_CONV_DIM + GDN_VAL_DIM:GDN_CONV_DIM + GDN_VAL_DIM + GDN_HEADS]
    a_raw = proj[..., GDN_CONV_DIM + GDN_VAL_DIM + GDN_HEADS:]
    qkv_c, new_buf = causal_conv(qkv, conv_buf, conv_w)
    qkv_c = jax.nn.silu(qkv_c)
    q = l2norm(qkv_c[..., :GDN_KEY_DIM].reshape(b, t, GDN_HEADS, GDN_DK))
    k = l2norm(qkv_c[..., GDN_KEY_DIM:2 * GDN_KEY_DIM].reshape(b, t, GDN_HEADS, GDN_DK))
    v = qkv_c[..., 2 * GDN_KEY_DIM:].reshape(b, t, GDN_HEADS, GDN_DV)
    beta = jax.nn.sigmoid(b_raw.astype(jnp.float32))
    g = -jnp.exp(a_log.astype(jnp.float32)) * jax.nn.softplus(a_raw.astype(jnp.float32) + dt_bias.astype(jnp.float32))
    o, s_new = gated_delta_rule(q, k, v, g, beta, s0)
    o = rmsnorm(o, norm_w) * jax.nn.silu(z.reshape(b, t, GDN_HEADS, GDN_DV))
    return o.reshape(b, t, GDN_VAL_DIM) @ w_o, new_buf, s_new


def gdn_layer(hp, hs, mem_state, conv_state, w_in, conv_w, a_log, dt_bias, norm_w, w_o):
    bp = hp.shape[0]
    buf0 = jnp.zeros((bp, GDN_CONV_W - 1, GDN_CONV_DIM), hp.dtype)
    s0 = jnp.zeros((bp, GDN_HEADS, GDN_DK, GDN_DV), mem_state.dtype)
    y_p, conv_p, mem_p = gdn_mixer(hp, buf0, s0, w_in, conv_w, a_log, dt_bias, norm_w, w_o)
    y_s, conv_s, mem_s = gdn_mixer(hs, conv_state, mem_state, w_in, conv_w, a_log, dt_bias, norm_w, w_o)
    return y_p, y_s, mem_p, conv_p, mem_s, conv_s


def setup_inputs(seed: int = 0) -> dict:
    key = jax.random.key(seed)
    ks = iter(jax.random.split(key, 48))
    n_pages = PAST_LEN // PAGE_SIZE
    n_used = DEC_BATCH * n_pages
    n_pool = n_used + n_used // 4

    def rnd(shape, scale=1.0):
        return scale * jax.random.normal(next(ks), shape, jnp.float32)

    def gain(shape):
        return 1.0 + 0.02 * rnd(shape)

    x_prompt = rnd((BATCH, SEQ, D_MODEL))
    x_sample = rnd((DEC_BATCH, DEC_SEQ, D_MODEL))
    cache_mla_ckv = rnd((N_MLA_LAYERS, n_pool, PAGE_SIZE, MLA_KV_LORA))
    cache_mla_krope = rnd((N_MLA_LAYERS, n_pool, PAGE_SIZE, MLA_D_ROPE))
    cache_sb_k = rnd((N_SB_LAYERS, n_pool, PAGE_SIZE, SB_HEADS, SB_HEAD_DIM))
    cache_sb_v = rnd((N_SB_LAYERS, n_pool, PAGE_SIZE, SB_HEADS, SB_HEAD_DIM))
    state_gdn_mem = rnd((N_GDN_LAYERS, DEC_BATCH, GDN_HEADS, GDN_DK, GDN_DV), 0.1)
    state_gdn_conv = rnd((N_GDN_LAYERS, DEC_BATCH, GDN_CONV_W - 1, GDN_CONV_DIM))
    page_table = jax.random.permutation(next(ks), n_pool)[:n_used].reshape(DEC_BATCH, n_pages).astype(jnp.int32)

    norm_ffn = gain((DEPTH, 2, D_MODEL))
    ffn_w_gate = rnd((DEPTH, 2, D_MODEL, D_FF), D_MODEL ** -0.5)
    ffn_w_up = rnd((DEPTH, 2, D_MODEL, D_FF), D_MODEL ** -0.5)
    ffn_w_down = rnd((DEPTH, 2, D_FF, D_MODEL), D_FF ** -0.5)
    norm_mix = gain((DEPTH, D_MODEL))
    norm_final = gain((D_MODEL,))

    mla_w_in = rnd((N_MLA_LAYERS, D_MODEL, MLA_IN_DIM), D_MODEL ** -0.5)
    mla_norm_q = gain((N_MLA_LAYERS, MLA_Q_LORA))
    mla_norm_kv = gain((N_MLA_LAYERS, MLA_KV_LORA))
    mla_w_uq = rnd((N_MLA_LAYERS, MLA_Q_LORA, MLA_HEADS, MLA_D_NOPE + MLA_D_ROPE), MLA_Q_LORA ** -0.5)
    mla_w_uk = rnd((N_MLA_LAYERS, MLA_KV_LORA, MLA_HEADS, MLA_D_NOPE), MLA_KV_LORA ** -0.5)
    mla_w_uv = rnd((N_MLA_LAYERS, MLA_KV_LORA, MLA_HEADS, MLA_D_V), MLA_KV_LORA ** -0.5)
    mla_w_o = rnd((N_MLA_LAYERS, MLA_HEADS * MLA_D_V, D_MODEL), (MLA_HEADS * MLA_D_V) ** -0.5)

    sb_w_in = rnd((N_SB_LAYERS, D_MODEL, 3 * SB_DIM), D_MODEL ** -0.5)
    sb_w_o = rnd((N_SB_LAYERS, SB_DIM, D_MODEL), SB_DIM ** -0.5)

    gdn_w_in = rnd((N_GDN_LAYERS, D_MODEL, GDN_IN_DIM), D_MODEL ** -0.5)
    gdn_conv_w = rnd((N_GDN_LAYERS, GDN_CONV_W, GDN_CONV_DIM), GDN_CONV_W ** -0.5)
    gdn_a_log = jnp.log(jax.random.uniform(next(ks), (N_GDN_LAYERS, GDN_HEADS), jnp.float32, 1.0, 16.0))
    dt = jnp.exp(jax.random.uniform(next(ks), (N_GDN_LAYERS, GDN_HEADS), jnp.float32,
                                    math.log(0.001), math.log(0.1)))
    gdn_dt_bias = dt + jnp.log(-jnp.expm1(-dt))
    gdn_norm_w = gain((N_GDN_LAYERS, GDN_DV))
    gdn_w_o = rnd((N_GDN_LAYERS, GDN_VAL_DIM, D_MODEL), GDN_VAL_DIM ** -0.5)

    return {
        'x_prompt': x_prompt, 'x_sample': x_sample,
        'cache_mla_ckv': cache_mla_ckv, 'cache_mla_krope': cache_mla_krope,
        'cache_sb_k': cache_sb_k, 'cache_sb_v': cache_sb_v,
        'state_gdn_mem': state_gdn_mem, 'state_gdn_conv': state_gdn_conv,
        'page_table': page_table,
        'norm_ffn': norm_ffn, 'ffn_w_gate': ffn_w_gate, 'ffn_w_up': ffn_w_up, 'ffn_w_down': ffn_w_down,
        'norm_mix': norm_mix, 'norm_final': norm_final,
        'mla_w_in': mla_w_in, 'mla_norm_q': mla_norm_q, 'mla_norm_kv': mla_norm_kv,
        'mla_w_uq': mla_w_uq, 'mla_w_uk': mla_w_uk, 'mla_w_uv': mla_w_uv, 'mla_w_o': mla_w_o,
        'sb_w_in': sb_w_in, 'sb_w_o': sb_w_o,
        'gdn_w_in': gdn_w_in, 'gdn_conv_w': gdn_conv_w, 'gdn_a_log': gdn_a_log,
        'gdn_dt_bias': gdn_dt_bias, 'gdn_norm_w': gdn_norm_w, 'gdn_w_o': gdn_w_o,
    }


def reference(x_prompt, x_sample, cache_mla_ckv, cache_mla_krope, cache_sb_k, cache_sb_v,
              state_gdn_mem, state_gdn_conv, page_table,
              norm_ffn, ffn_w_gate, ffn_w_up, ffn_w_down, norm_mix, norm_final,
              mla_w_in, mla_norm_q, mla_norm_kv, mla_w_uq, mla_w_uk, mla_w_uv, mla_w_o,
              sb_w_in, sb_w_o,
              gdn_w_in, gdn_conv_w, gdn_a_log, gdn_dt_bias, gdn_norm_w, gdn_w_o):
    xp, xs = x_prompt, x_sample
    mla_ckv_p, mla_kr_p, mla_ckv_s, mla_kr_s = [], [], [], []
    sb_k_p, sb_v_p, sb_k_s, sb_v_s = [], [], [], []
    gdn_mem_p, gdn_conv_p, gdn_mem_s, gdn_conv_s = [], [], [], []
    for layer in range(DEPTH):
        kind = layer % N_MIXERS
        idx = layer // N_MIXERS
        ffn0 = (norm_ffn[layer, 0], ffn_w_gate[layer, 0], ffn_w_up[layer, 0], ffn_w_down[layer, 0])
        ffn1 = (norm_ffn[layer, 1], ffn_w_gate[layer, 1], ffn_w_up[layer, 1], ffn_w_down[layer, 1])
        xp, xs = half_ffn(xp, *ffn0), half_ffn(xs, *ffn0)
        hp, hs = rmsnorm(xp, norm_mix[layer]), rmsnorm(xs, norm_mix[layer])
        if kind == 0:
            yp, ys, a, b, c, d = mla_layer(hp, hs, cache_mla_ckv[idx], cache_mla_krope[idx], page_table,
                                           mla_w_in[idx], mla_norm_q[idx], mla_norm_kv[idx],
                                           mla_w_uq[idx], mla_w_uk[idx], mla_w_uv[idx], mla_w_o[idx])
            mla_ckv_p.append(a); mla_kr_p.append(b); mla_ckv_s.append(c); mla_kr_s.append(d)
        elif kind == 1:
            yp, ys, a, b, c, d = sb_layer(hp, hs, cache_sb_k[idx], cache_sb_v[idx], page_table,
                                          sb_w_in[idx], sb_w_o[idx])
            sb_k_p.append(a); sb_v_p.append(b); sb_k_s.append(c); sb_v_s.append(d)
        else:
            yp, ys, a, b, c, d = gdn_layer(hp, hs, state_gdn_mem[idx], state_gdn_conv[idx],
                                           gdn_w_in[idx], gdn_conv_w[idx], gdn_a_log[idx],
                                           gdn_dt_bias[idx], gdn_norm_w[idx], gdn_w_o[idx])
            gdn_mem_p.append(a); gdn_conv_p.append(b); gdn_mem_s.append(c); gdn_conv_s.append(d)
        xp, xs = xp + yp, xs + ys
        xp, xs = half_ffn(xp, *ffn1), half_ffn(xs, *ffn1)
    y_prompt = rmsnorm(xp, norm_final)
    y_sample = rmsnorm(xs, norm_final)
    return (y_prompt, y_sample,
            jnp.stack(mla_ckv_p), jnp.stack(mla_kr_p), jnp.stack(sb_k_p), jnp.stack(sb_v_p),
            jnp.stack(gdn_mem_p), jnp.stack(gdn_conv_p),
            jnp.stack(mla_ckv_s), jnp.stack(mla_kr_s), jnp.stack(sb_k_s), jnp.stack(sb_v_s),
            jnp.stack(gdn_mem_s), jnp.stack(gdn_conv_s))
```

```python
import functools
import math

import jax
import jax.numpy as jnp
from jax import lax
from jax.experimental import pallas as pl
from jax.experimental.pallas import tpu as pltpu

F32 = jnp.float32
BF16 = jnp.bfloat16

NORM_EPS = 1e-6
ROPE_THETA = 10000.0
LANE = 128
VMEM_LIMIT = 56 * 1024 * 1024

MLA_D_NOPE = 128
MLA_D_ROPE = 64
GDN_CHUNK = 128
GDN_INV_BLOCK = 16
FFN_CHUNK = 256


def _params(*sem):
    return pltpu.CompilerParams(dimension_semantics=sem, vmem_limit_bytes=VMEM_LIMIT)


def _tile(m, pref=512):
    t = pref
    while m % t:
        t //= 2
    return t


def _rms(x, w):
    return x * lax.rsqrt(jnp.mean(x * x, axis=-1, keepdims=True) + NORM_EPS) * w


def _dot(a, b):
    return jnp.dot(a, b, preferred_element_type=F32)


def _dot_nt(a, b):
    return lax.dot_general(a, b, (((1,), (1,)), ((), ())), preferred_element_type=F32)


def _softplus(z):
    return jnp.maximum(z, 0.0) + jnp.log(1.0 + jnp.exp(-jnp.abs(z)))


def _split2(x):
    hi = x.astype(BF16)
    lo = (x - hi.astype(F32)).astype(BF16)
    return hi, lo


def _dot3(a, b):
    a_hi, a_lo = _split2(a)
    b_hi, b_lo = _split2(b)
    return _dot(a_hi, b_hi) + _dot(a_hi, b_lo) + _dot(a_lo, b_hi)


def _split3(x):
    hi = x.astype(BF16)
    r = x - hi.astype(F32)
    mid = r.astype(BF16)
    lo = (r - mid.astype(F32)).astype(BF16)
    return hi, mid, lo


def _full(shape):
    n = len(shape)
    return pl.BlockSpec(shape, lambda *_: (0,) * n)


def _ffn_body(x_ref, g_ref, wg_ref, wu_ref, wd_ref, *rest, n_chunks, final):
    if final:
        gf_ref, o_ref = rest
    else:
        (o_ref,) = rest
    x = x_ref[...]
    h = _rms(x, g_ref[...]).astype(BF16)
    acc = jnp.zeros_like(x)
    for c in range(n_chunks):
        sl = slice(c * FFN_CHUNK, (c + 1) * FFN_CHUNK)
        gate = _dot(h, wg_ref[:, sl])
        up = _dot(h, wu_ref[:, sl])
        a = (gate * jax.nn.sigmoid(gate) * up).astype(BF16)
        acc = acc + _dot(a, wd_ref[sl, :])
    y = x + 0.5 * acc
    if final:
        y = _rms(y, gf_ref[...])
    o_ref[...] = y


def _ffn_half(x, g, wg, wu, wd, final_g=None):
    m, d = x.shape
    f = wg.shape[1]
    tm = _tile(m)
    final = final_g is not None
    ins = [x, g.reshape(1, d), wg, wu, wd]
    specs = [pl.BlockSpec((tm, d), lambda i: (i, 0)), _full((1, d)),
             _full((d, f)), _full((d, f)), _full((f, d))]
    if final:
        ins.append(final_g.reshape(1, d))
        specs.append(_full((1, d)))
    return pl.pallas_call(
        functools.partial(_ffn_body, n_chunks=f // FFN_CHUNK, final=final),
        out_shape=jax.ShapeDtypeStruct((m, d), F32),
        grid=(m // tm,),
        in_specs=specs,
        out_specs=pl.BlockSpec((tm, d), lambda i: (i, 0)),
        compiler_params=_params("parallel"),
        name="ffn_half",
    )(*ins)


def _oproj_body(x_ref, o_ref, w_ref, y_ref):
    y_ref[...] = x_ref[...] + _dot(o_ref[...], w_ref[...])


def _out_proj(x, o, w):
    m, d = x.shape
    k = o.shape[1]
    tm = _tile(m)
    return pl.pallas_call(
        _oproj_body,
        out_shape=jax.ShapeDtypeStruct((m, d), F32),
        grid=(m // tm,),
        in_specs=[pl.BlockSpec((tm, d), lambda i: (i, 0)),
                  pl.BlockSpec((tm, k), lambda i: (i, 0)), _full((k, d))],
        out_specs=pl.BlockSpec((tm, d), lambda i: (i, 0)),
        compiler_params=_params("parallel"),
        name="out_proj",
    )(x, o, w)


def _rot_cols(w):
    half = w.shape[-1] // 2
    return jnp.concatenate([-w[..., half:], w[..., :half]], axis=-1)


def _rope_tables(pos):
    half = MLA_D_ROPE // 2
    inv = ROPE_THETA ** (-jnp.arange(half, dtype=F32) / half)
    ang = pos.astype(F32)[:, None] * inv[None, :]
    return jnp.tile(jnp.cos(ang), (1, 4)), jnp.tile(jnp.sin(ang), (1, 4))


def _mla_weights(w_in, w_uq, w_uk, w_uv, w_o):
    ql, kvl = w_uq.shape[0], w_uk.shape[0]
    heads = w_uq.shape[1]
    wq, wkv, wkr = w_in[:, :ql], w_in[:, ql:ql + kvl], w_in[:, ql + kvl:]
    wkr_rot = _rot_cols(wkr)
    w_in_ext = jnp.concatenate([wq, wkv, wkr, wkr, wkr_rot, wkr_rot], axis=1).astype(BF16)
    w_nope = w_uq[:, :, :MLA_D_NOPE].reshape(ql, heads * MLA_D_NOPE)
    w_rope = w_uq[:, :, MLA_D_NOPE:]
    w_uq_all = jnp.concatenate(
        [w_nope, w_rope.reshape(ql, heads * MLA_D_ROPE),
         _rot_cols(w_rope).reshape(ql, heads * MLA_D_ROPE)], axis=1).astype(BF16)
    w_uk_t = jnp.transpose(w_uk, (1, 2, 0)).astype(BF16)
    w_uv_h = jnp.transpose(w_uv, (1, 0, 2)).astype(BF16)
    return w_in_ext, w_uq_all, w_uk_t, w_uv_h, w_o.astype(BF16)


def _mla_proj_body(x_ref, gm_ref, win_ref, nq_ref, nkv_ref, wuq_ref, wuk_ref, cos_ref, sin_ref,
                   q_ref, ckv_ref, kr_ref, kcat_ref, *, heads, scale):
    ql = nq_ref.shape[1]
    kvl = nkv_ref.shape[1]
    h = _rms(x_ref[...], gm_ref[...]).astype(BF16)
    proj = _dot(h, win_ref[...])
    cq = _rms(proj[:, :ql], nq_ref[...]).astype(BF16)
    ckv = _rms(proj[:, ql:ql + kvl], nkv_ref[...])
    cos = cos_ref[...]
    sin = sin_ref[...]
    o = ql + kvl
    kr2 = proj[:, o:o + LANE] * cos + proj[:, o + LANE:o + 2 * LANE] * sin
    ckv_ref[...] = ckv
    kr_ref[...] = kr2[:, :MLA_D_ROPE]
    kcat_ref[:, :kvl] = ckv.astype(BF16)
    kcat_ref[:, kvl:] = kr2.astype(BF16)
    lane = lax.broadcasted_iota(jnp.int32, cos.shape, 1)
    first = lane < MLA_D_ROPE
    nope_cols = heads * MLA_D_NOPE
    rope_cols = heads * MLA_D_ROPE
    for j in range(heads // 2):
        c0 = nope_cols + j * LANE
        qr = _dot(cq, wuq_ref[:, c0:c0 + LANE])
        qrot = _dot(cq, wuq_ref[:, c0 + rope_cols:c0 + rope_cols + LANE])
        qr2 = (qr * cos + qrot * sin) * scale
        for hh in range(2):
            hd = 2 * j + hh
            qn = _dot(cq, wuq_ref[:, hd * MLA_D_NOPE:(hd + 1) * MLA_D_NOPE]).astype(BF16)
            qlat = _dot(qn, wuk_ref[hd]) * scale
            q_ref[hd, :, :kvl] = qlat.astype(BF16)
            keep = first if hh == 0 else jnp.logical_not(first)
            q_ref[hd, :, kvl:] = jnp.where(keep, qr2, 0.0).astype(BF16)


def _mla_proj(x, g_mix, w_in_ext, norm_q, norm_kv, w_uq_all, w_uk_t, cos, sin):
    m, d = x.shape
    heads, _, kvl = w_uk_t.shape
    ql = norm_q.shape[0]
    tm = _tile(m)
    scale = (MLA_D_NOPE + MLA_D_ROPE) ** -0.5
    qw = kvl + LANE
    row = lambda i: (i, 0)
    return pl.pallas_call(
        functools.partial(_mla_proj_body, heads=heads, scale=scale),
        out_shape=(jax.ShapeDtypeStruct((heads, m, qw), BF16),
                   jax.ShapeDtypeStruct((m, kvl), F32),
                   jax.ShapeDtypeStruct((m, MLA_D_ROPE), F32),
                   jax.ShapeDtypeStruct((m, qw), BF16)),
        grid=(m // tm,),
        in_specs=[pl.BlockSpec((tm, d), row), _full((1, d)), _full(w_in_ext.shape),
                  _full((1, ql)), _full((1, kvl)), _full(w_uq_all.shape), _full(w_uk_t.shape),
                  pl.BlockSpec((tm, LANE), row), pl.BlockSpec((tm, LANE), row)],
        out_specs=(pl.BlockSpec((heads, tm, qw), lambda i: (0, i, 0)),
                   pl.BlockSpec((tm, kvl), row), pl.BlockSpec((tm, MLA_D_ROPE), row),
                   pl.BlockSpec((tm, qw), row)),
        compiler_params=_params("parallel"),
        name="mla_proj",
    )(x, g_mix.reshape(1, d), w_in_ext, norm_q.reshape(1, ql), norm_kv.reshape(1, kvl),
      w_uq_all, w_uk_t, cos, sin)


def _mla_prompt_body(q_ref, k_ref, o_ref, m_sc, l_sc, acc_sc, *, tq, tk, kvl):
    i = pl.program_id(1)
    rows = m_sc.shape[0]
    m_sc[...] = jnp.full(m_sc.shape, -jnp.inf, F32)
    l_sc[...] = jnp.zeros(l_sc.shape, F32)
    acc_sc[...] = jnp.zeros(acc_sc.shape, F32)

    def step(j, masked):
        q = q_ref[...].reshape(rows, q_ref.shape[-1])
        k = k_ref[0, pl.ds(pl.multiple_of(j * tk, tk), tk), :]
        s = _dot_nt(q, k)
        if masked:
            qpos = i * tq + (lax.broadcasted_iota(jnp.int32, s.shape, 0) & (tq - 1))
            kpos = j * tk + lax.broadcasted_iota(jnp.int32, s.shape, 1)
            s = jnp.where(kpos <= qpos, s, -1e30)
        m_prev = m_sc[...]
        m_new = jnp.maximum(m_prev, jnp.max(s, axis=1, keepdims=True))
        alpha = jnp.exp(m_prev - m_new)
        p = jnp.exp(s - m_new)
        l_sc[...] = alpha * l_sc[...] + jnp.sum(p, axis=1, keepdims=True)
        acc_sc[...] = alpha * acc_sc[...] + _dot(p.astype(BF16), k[:, :kvl])
        m_sc[...] = m_new

    n_full = (i * tq + 1) // tk

    def loop_body(j, carry):
        step(j, False)
        return carry

    lax.fori_loop(0, n_full, loop_body, 0)
    step(n_full, True)
    o = acc_sc[...] / l_sc[...]
    o_ref[...] = o.astype(o_ref.dtype).reshape(o_ref.shape)


def _mla_prompt_attn(q, kcat, b, t, kvl):
    heads, m, qw = q.shape
    tq = 128
    tk = min(512, t)
    nq = t // tq
    q4 = q.reshape(heads, b, t, qw)
    k3 = kcat.reshape(b, t, qw)
    out = pl.pallas_call(
        functools.partial(_mla_prompt_body, tq=tq, tk=tk, kvl=kvl),
        out_shape=jax.ShapeDtypeStruct((heads, b, t, kvl), BF16),
        grid=(b, nq),
        in_specs=[pl.BlockSpec((heads, 1, tq, qw), lambda bi, i: (0, bi, i, 0)),
                  pl.BlockSpec((1, t, qw), lambda bi, i: (bi, 0, 0))],
        out_specs=pl.BlockSpec((heads, 1, tq, kvl), lambda bi, i: (0, bi, i, 0)),
        scratch_shapes=[pltpu.VMEM((heads * tq, 1), F32), pltpu.VMEM((heads * tq, 1), F32),
                        pltpu.VMEM((heads * tq, kvl), F32)],
        compiler_params=_params("parallel", "arbitrary"),
        name="mla_prompt_attn",
    )(q4, k3)
    return out.reshape(heads, m, kvl)


def _mla_decode_body(pt_ref, ql_ref, qr_ref, *refs, n_pg, t_new):
    del pt_ref
    ckv_refs = refs[:n_pg]
    kr_refs = refs[n_pg:2 * n_pg]
    nl_ref, nr_ref, o_ref, m_sc, l_sc, acc_sc = refs[2 * n_pg:]
    j = pl.program_id(1)

    @pl.when(j == 0)
    def _():
        m_sc[...] = jnp.full(m_sc.shape, -jnp.inf, F32)
        l_sc[...] = jnp.zeros(l_sc.shape, F32)
        acc_sc[...] = jnp.zeros(acc_sc.shape, F32)

    ql = ql_ref[0]
    qr = qr_ref[0]

    def update(s_parts, v_parts):
        s = s_parts[0] if len(s_parts) == 1 else jnp.concatenate(s_parts, axis=1)
        m_prev = m_sc[...]
        m_new = jnp.maximum(m_prev, jnp.max(s, axis=1, keepdims=True))
        alpha = jnp.exp(m_prev - m_new)
        p = jnp.exp(s - m_new)
        l_sc[...] = alpha * l_sc[...] + jnp.sum(p, axis=1, keepdims=True)
        pv = None
        for n, v in enumerate(v_parts):
            w = p.shape[1] // len(v_parts)
            part = _dot(p[:, n * w:(n + 1) * w].astype(BF16), v)
            pv = part if pv is None else pv + part
        acc_sc[...] = alpha * acc_sc[...] + pv
        m_sc[...] = m_new

    s_parts, v_parts = [], []
    for n in range(n_pg):
        kp = ckv_refs[n][...].astype(BF16)
        rp = kr_refs[n][...].astype(BF16)
        s_parts.append(_dot_nt(ql, kp) + _dot_nt(qr, rp))
        v_parts.append(kp)
    update(s_parts, v_parts)

    @pl.when(j == pl.num_programs(1) - 1)
    def _():
        kn = nl_ref[0]
        s = _dot_nt(ql, kn) + _dot_nt(qr, nr_ref[0])
        tq = lax.broadcasted_iota(jnp.int32, s.shape, 0) % t_new
        col = lax.broadcasted_iota(jnp.int32, s.shape, 1)
        s = jnp.where(col <= tq, s, -1e30)
        update([s], [kn])
        o_ref[0] = (acc_sc[...] / l_sc[...]).astype(o_ref.dtype)


def _mla_decode_attn(q_lat, q_rope, cache_ckv, cache_kr, layer, page_table, new_lat, new_rope, t_new):
    b, rows, kvl = q_lat.shape
    n_pages = page_table.shape[1]
    page = cache_ckv.shape[2]
    n_pg = 8
    while n_pages % n_pg:
        n_pg //= 2

    def page_spec(width, n):
        return pl.BlockSpec((None, None, page, width),
                            lambda bi, j, pt: (layer, pt[bi, j * n_pg + n], 0, 0))

    per_b = lambda bi, j, pt: (bi, 0, 0)
    grid_spec = pltpu.PrefetchScalarGridSpec(
        num_scalar_prefetch=1,
        grid=(b, n_pages // n_pg),
        in_specs=([pl.BlockSpec((1, rows, kvl), per_b), pl.BlockSpec((1, rows, MLA_D_ROPE), per_b)]
                  + [page_spec(kvl, n) for n in range(n_pg)]
                  + [page_spec(MLA_D_ROPE, n) for n in range(n_pg)]
                  + [pl.BlockSpec((1, page, kvl), per_b), pl.BlockSpec((1, page, MLA_D_ROPE), per_b)]),
        out_specs=pl.BlockSpec((1, rows, kvl), per_b),
        scratch_shapes=[pltpu.VMEM((rows, 1), F32), pltpu.VMEM((rows, 1), F32),
                        pltpu.VMEM((rows, kvl), F32)],
    )
    return pl.pallas_call(
        functools.partial(_mla_decode_body, n_pg=n_pg, t_new=t_new),
        out_shape=jax.ShapeDtypeStruct((b, rows, kvl), BF16),
        grid_spec=grid_spec,
        compiler_params=_params("parallel", "arbitrary"),
        name="mla_decode_attn",
    )(page_table, q_lat, q_rope, *([cache_ckv] * n_pg), *([cache_kr] * n_pg), new_lat, new_rope)


def _mla_out_body(x_ref, o_ref, wuv_ref, wo_ref, y_ref, *, heads):
    dv = wuv_ref.shape[2]
    acc = x_ref[...]
    for j in range(heads // 2):
        a = _dot(o_ref[2 * j], wuv_ref[2 * j]).astype(BF16)
        bb = _dot(o_ref[2 * j + 1], wuv_ref[2 * j + 1]).astype(BF16)
        pair = jnp.concatenate([a, bb], axis=1)
        acc = acc + _dot(pair, wo_ref[2 * j * dv:(2 * j + 2) * dv, :])
    y_ref[...] = acc


def _mla_out(x, o_lat, w_uv_h, w_o):
    m, d = x.shape
    heads, _, kvl = o_lat.shape
    tm = _tile(m)
    return pl.pallas_call(
        functools.partial(_mla_out_body, heads=heads),
        out_shape=jax.ShapeDtypeStruct((m, d), F32),
        grid=(m // tm,),
        in_specs=[pl.BlockSpec((tm, d), lambda i: (i, 0)),
                  pl.BlockSpec((heads, tm, kvl), lambda i: (0, i, 0)),
                  _full(w_uv_h.shape), _full(w_o.shape)],
        out_specs=pl.BlockSpec((tm, d), lambda i: (i, 0)),
        compiler_params=_params("parallel"),
        name="mla_out",
    )(x, o_lat, w_uv_h, w_o)


def _mla_layer(xp, xs, bp, tp, bs, ts, cache_ckv, cache_kr, layer, page_table, g_mix,
               w_in, norm_q, norm_kv, w_uq, w_uk, w_uv, w_o):
    w_in_ext, w_uq_all, w_uk_t, w_uv_h, w_o_b = _mla_weights(w_in, w_uq, w_uk, w_uv, w_o)
    heads, _, kvl = w_uk_t.shape
    page = cache_ckv.shape[2]
    past = page_table.shape[1] * page
    cos_p, sin_p = _rope_tables(jnp.tile(jnp.arange(tp, dtype=jnp.int32), bp))
    q_p, ckv_p, kr_p, kcat_p = _mla_proj(xp, g_mix, w_in_ext, norm_q, norm_kv, w_uq_all, w_uk_t, cos_p, sin_p)
    ol_p = _mla_prompt_attn(q_p, kcat_p, bp, tp, kvl)
    yp = _mla_out(xp, ol_p, w_uv_h, w_o_b)
    cos_s, sin_s = _rope_tables(jnp.tile(past + jnp.arange(ts, dtype=jnp.int32), bs))
    q_s, ckv_s, kr_s, kcat_s = _mla_proj(xs, g_mix, w_in_ext, norm_q, norm_kv, w_uq_all, w_uk_t, cos_s, sin_s)
    q_s = jnp.transpose(q_s.reshape(heads, bs, ts, kvl + LANE), (1, 0, 2, 3)).reshape(bs, heads * ts, kvl + LANE)
    q_lat = q_s[..., :kvl]
    q_rope = q_s[..., kvl:kvl + MLA_D_ROPE] + q_s[..., kvl + MLA_D_ROPE:]
    kcat_s = kcat_s.reshape(bs, ts, kvl + LANE)
    new_lat = jnp.pad(kcat_s[..., :kvl], ((0, 0), (0, page - ts), (0, 0)))
    new_rope = jnp.pad(kcat_s[..., kvl:kvl + MLA_D_ROPE], ((0, 0), (0, page - ts), (0, 0)))
    ol_s = _mla_decode_attn(q_lat, q_rope, cache_ckv, cache_kr, layer, page_table, new_lat, new_rope, ts)
    ol_s = jnp.transpose(ol_s.reshape(bs, heads, ts, kvl), (1, 0, 2, 3)).reshape(heads, bs * ts, kvl)
    ys = _mla_out(xs, ol_s, w_uv_h, w_o_b)
    return (yp, ys, ckv_p.reshape(bp, tp, kvl), kr_p.reshape(bp, tp, MLA_D_ROPE),
            ckv_s.reshape(bs, ts, kvl), kr_s.reshape(bs, ts, MLA_D_ROPE))


def _sb_proj_body(x_ref, g_ref, w_ref, q_ref, k_ref, v_ref, kb_ref, vb_ref, *, scale):
    n = q_ref.shape[1]
    h = _rms(x_ref[...], g_ref[...]).astype(BF16)
    q_ref[...] = (_dot(h, w_ref[:, :n]) * scale).astype(BF16)
    k = _dot(h, w_ref[:, n:2 * n])
    k_ref[...] = k
    kb_ref[...] = k.astype(BF16)
    v = _dot(h, w_ref[:, 2 * n:])
    v_ref[...] = v
    vb_ref[...] = v.astype(BF16)


def _sb_proj(x, g_mix, w_in, scale):
    m, d = x.shape
    n = w_in.shape[1] // 3
    tm = _tile(m)
    row = lambda i: (i, 0)
    return pl.pallas_call(
        functools.partial(_sb_proj_body, scale=scale),
        out_shape=(jax.ShapeDtypeStruct((m, n), BF16), jax.ShapeDtypeStruct((m, n), F32),
                   jax.ShapeDtypeStruct((m, n), F32), jax.ShapeDtypeStruct((m, n), BF16),
                   jax.ShapeDtypeStruct((m, n), BF16)),
        grid=(m // tm,),
        in_specs=[pl.BlockSpec((tm, d), row), _full((1, d)), _full(w_in.shape)],
        out_specs=tuple(pl.BlockSpec((tm, n), row) for _ in range(5)),
        compiler_params=_params("parallel"),
        name="sb_proj",
    )(x, g_mix.reshape(1, d), w_in)


def _sb_prompt_body(q_ref, k_ref, v_ref, u_ref, o_ref, acc_sc, car_sc, *, tq, hd):
    i = pl.program_id(2)
    acc_sc[...] = jnp.zeros(acc_sc.shape, F32)
    car_sc[...] = jnp.zeros(car_sc.shape, F32)
    q2 = q_ref[0]
    lane = lax.broadcasted_iota(jnp.int32, q2.shape, 1)
    zero = jnp.zeros_like(q2)
    q_heads = (jnp.where(lane < hd, q2, zero), jnp.where(lane >= hd, q2, zero))

    def block(j, masked):
        start = pl.multiple_of(j * tq, tq)
        k = k_ref[0, pl.ds(start, tq), :]
        v = v_ref[0, pl.ds(start, tq), :]
        u = u_ref[...]
        for n in range(2):
            z = _dot_nt(q_heads[n], k)
            sp = _softplus(z)
            log_keep = -sp
            if masked:
                causal = (lax.broadcasted_iota(jnp.int32, z.shape, 1)
                          < lax.broadcasted_iota(jnp.int32, z.shape, 0))
                log_keep = jnp.where(causal, log_keep, 0.0)
            hi, lo = _split2(log_keep)
            later = _dot(hi, u) + _dot(lo, u) + car_sc[n]
            a = jnp.exp(z - sp + later)
            if masked:
                a = jnp.where(causal, a, 0.0)
            acc_sc[n] = acc_sc[n] + _dot(a.astype(BF16), v)
            car_sc[n] = car_sc[n] + jnp.sum(log_keep, axis=1, keepdims=True)

    block(i, True)

    def loop_body(jj, carry):
        block(i - 1 - jj, False)
        return carry

    lax.fori_loop(0, i, loop_body, 0)
    lane_o = lax.broadcasted_iota(jnp.int32, acc_sc.shape[1:], 1)
    o_ref[0] = jnp.where(lane_o < hd, acc_sc[0], acc_sc[1]).astype(o_ref.dtype)


def _strict_later(n):
    r = lax.broadcasted_iota(jnp.int32, (n, n), 0)
    c = lax.broadcasted_iota(jnp.int32, (n, n), 1)
    return (r > c).astype(BF16)


def _sb_prompt_attn(q, kb, vb, b, t, hd):
    m, n = q.shape
    tq = min(256, t)
    nq = t // tq
    pairs = n // LANE
    out = pl.pallas_call(
        functools.partial(_sb_prompt_body, tq=tq, hd=hd),
        out_shape=jax.ShapeDtypeStruct((b, t, n), BF16),
        grid=(b, pairs, nq),
        in_specs=[pl.BlockSpec((1, tq, LANE), lambda bi, p, i: (bi, i, p)),
                  pl.BlockSpec((1, t, LANE), lambda bi, p, i: (bi, 0, p)),
                  pl.BlockSpec((1, t, LANE), lambda bi, p, i: (bi, 0, p)),
                  _full((tq, tq))],
        out_specs=pl.BlockSpec((1, tq, LANE), lambda bi, p, i: (bi, i, p)),
        scratch_shapes=[pltpu.VMEM((2, tq, LANE), F32), pltpu.VMEM((2, tq, 1), F32)],
        compiler_params=_params("parallel", "parallel", "arbitrary"),
        name="sb_prompt_attn",
    )(q.reshape(b, t, n), kb.reshape(b, t, n), vb.reshape(b, t, n), _strict_later(tq))
    return out.reshape(m, n)


def _sb_decode_body(pt_ref, qbd_ref, *refs, n_pg, t_new):
    del pt_ref
    k_refs = refs[:n_pg]
    v_refs = refs[n_pg:2 * n_pg]
    kn_ref, vn_ref, l_ref, o_ref, acc_sc, car_sc = refs[2 * n_pg:]
    jj = pl.program_id(1)
    qbd = qbd_ref[0]

    def process(k, v, masked):
        z = _dot(k, qbd)
        sp = _softplus(z)
        log_keep = -sp
        if masked:
            causal = (lax.broadcasted_iota(jnp.int32, z.shape, 0)
                      < lax.broadcasted_iota(jnp.int32, z.shape, 1) % t_new)
            log_keep = jnp.where(causal, log_keep, 0.0)
        hi, lo = _split2(log_keep)
        lm = l_ref[...]
        later = _dot(lm, hi) + _dot(lm, lo) + car_sc[...]
        a = jnp.exp(z - sp + later)
        if masked:
            a = jnp.where(causal, a, 0.0)
        acc_sc[...] = acc_sc[...] + lax.dot_general(
            a.astype(BF16), v, (((0,), (0,)), ((), ())), preferred_element_type=F32)
        car_sc[...] = car_sc[...] + jnp.sum(log_keep, axis=0, keepdims=True)

    @pl.when(jj == 0)
    def _():
        acc_sc[...] = jnp.zeros(acc_sc.shape, F32)
        car_sc[...] = jnp.zeros(car_sc.shape, F32)
        process(kn_ref[0], vn_ref[0], True)

    for n in range(n_pg):
        process(k_refs[n][...].astype(BF16), v_refs[n][...].astype(BF16), False)

    @pl.when(jj == pl.num_programs(1) - 1)
    def _():
        o_ref[0] = acc_sc[...]


def _sb_decode_attn(qbd, cache_k, cache_v, layer, page_table, k_new, v_new, t_new):
    b, n, cols = qbd.shape
    n_pages = page_table.shape[1]
    page = cache_k.shape[2]
    n_pg = 4
    while n_pages % n_pg:
        n_pg //= 2

    def page_spec(i):
        return pl.BlockSpec((None, None, page, n),
                            lambda bi, j, pt: (layer, pt[bi, n_pages - 1 - (j * n_pg + i)], 0, 0))

    per_b = lambda bi, j, pt: (bi, 0, 0)
    r = lax.broadcasted_iota(jnp.int32, (page, page), 0)
    c = lax.broadcasted_iota(jnp.int32, (page, page), 1)
    later_mat = (c > r).astype(BF16)
    grid_spec = pltpu.PrefetchScalarGridSpec(
        num_scalar_prefetch=1,
        grid=(b, n_pages // n_pg),
        in_specs=([pl.BlockSpec((1, n, cols), per_b)]
                  + [page_spec(i) for i in range(n_pg)] + [page_spec(i) for i in range(n_pg)]
                  + [pl.BlockSpec((1, page, n), per_b), pl.BlockSpec((1, page, n), per_b),
                     pl.BlockSpec((page, page), lambda bi, j, pt: (0, 0))]),
        out_specs=pl.BlockSpec((1, cols, n), per_b),
        scratch_shapes=[pltpu.VMEM((cols, n), F32), pltpu.VMEM((1, cols), F32)],
    )
    return pl.pallas_call(
        functools.partial(_sb_decode_body, n_pg=n_pg, t_new=t_new),
        out_shape=jax.ShapeDtypeStruct((b, cols, n), F32),
        grid_spec=grid_spec,
        compiler_params=_params("parallel", "arbitrary"),
        name="sb_decode_attn",
    )(page_table, qbd, *([cache_k] * n_pg), *([cache_v] * n_pg), k_new, v_new, later_mat)


def _sb_layer(xp, xs, bp, tp, bs, ts, cache_k, cache_v, layer, page_table, g_mix, w_in, w_o):
    heads, hd = cache_k.shape[3], cache_k.shape[4]
    n = heads * hd
    page = cache_k.shape[2]
    scale = hd ** -0.5
    w_in_b = w_in.astype(BF16)
    w_o_b = w_o.astype(BF16)
    q_p, k_p, v_p, kb_p, vb_p = _sb_proj(xp, g_mix, w_in_b, scale)
    o_p = _sb_prompt_attn(q_p, kb_p, vb_p, bp, tp, hd)
    yp = _out_proj(xp, o_p, w_o_b)
    q_s, k_s, v_s, kb_s, vb_s = _sb_proj(xs, g_mix, w_in_b, scale)
    q4 = q_s.reshape(bs, ts, heads, hd)
    eye = jnp.eye(heads, dtype=BF16)
    qbd = (q4[:, :, :, :, None] * eye[None, None, :, None, :])
    qbd = jnp.transpose(qbd, (0, 2, 3, 4, 1)).reshape(bs, n, heads * ts)
    k_new = jnp.pad(kb_s.reshape(bs, ts, n), ((0, 0), (0, page - ts), (0, 0)))
    v_new = jnp.pad(vb_s.reshape(bs, ts, n), ((0, 0), (0, page - ts), (0, 0)))
    ck = cache_k.reshape(cache_k.shape[:3] + (n,))
    cv = cache_v.reshape(cache_v.shape[:3] + (n,))
    res = _sb_decode_attn(qbd, ck, cv, layer, page_table, k_new, v_new, ts)
    res = res.reshape(bs, heads, ts, heads, hd)
    idx = jnp.arange(heads)
    o_s = res[:, idx, :, idx, :]
    o_s = jnp.transpose(o_s, (1, 2, 0, 3)).reshape(bs * ts, n).astype(BF16)
    ys = _out_proj(xs, o_s, w_o_b)
    return (yp, ys, k_p.reshape(bp, tp, heads, hd), v_p.reshape(bp, tp, heads, hd),
            k_s.reshape(bs, ts, heads, hd), v_s.reshape(bs, ts, heads, hd))


def _gdn_proj_body(x_ref, g_ref, w_ref, wbat_ref, qkv_ref, z_ref, ba_ref, bat_ref):
    nq = qkv_ref.shape[1]
    nz = z_ref.shape[1]
    h = _rms(x_ref[...], g_ref[...]).astype(BF16)
    for c0 in range(0, nq, 1024):
        qkv_ref[:, c0:c0 + 1024] = _dot(h, w_ref[:, c0:c0 + 1024])
    z_ref[...] = _dot(h, w_ref[:, nq:nq + nz])
    ba_ref[...] = _dot(h, w_ref[:, nq + nz:])
    bat_ref[...] = _dot_nt(wbat_ref[...], h)


def _gdn_proj(x, g_mix, w_pad, w_ba_t, nq, nz):
    m, d = x.shape
    tm = _tile(m)
    row = lambda i: (i, 0)
    return pl.pallas_call(
        _gdn_proj_body,
        out_shape=(jax.ShapeDtypeStruct((m, nq), F32), jax.ShapeDtypeStruct((m, nz), F32),
                   jax.ShapeDtypeStruct((m, LANE), F32), jax.ShapeDtypeStruct((LANE, m), F32)),
        grid=(m // tm,),
        in_specs=[pl.BlockSpec((tm, d), row), _full((1, d)), _full(w_pad.shape), _full(w_ba_t.shape)],
        out_specs=(pl.BlockSpec((tm, nq), row), pl.BlockSpec((tm, nz), row),
                   pl.BlockSpec((tm, LANE), row), pl.BlockSpec((LANE, tm), lambda i: (0, i))),
        compiler_params=_params("parallel"),
        name="gdn_proj",
    )(x, g_mix.reshape(1, d), w_pad, w_ba_t)


def _gdn_body(qkv_ref, z_ref, ba_ref, bat_ref, buf_ref, s0_ref, cw_ref, vec_ref, cols_ref,
              o_ref, s_ref, ext_sc, *, c, valid, heads, dk, conv_w):
    ci = pl.program_id(1)
    kdim = heads * dk
    dv = s_ref.shape[3]

    @pl.when(ci == 0)
    def _():
        ext_sc[0:8, :] = buf_ref[0]
        s_ref[...] = s0_ref[...]

    ext_sc[8:8 + c, :] = qkv_ref[0]
    y = None
    for i in range(conv_w):
        lo = 8 - (conv_w - 1) + i
        term = ext_sc[lo:lo + c, :] * cw_ref[i:i + 1, :]
        y = term if y is None else y + term
    ext_sc[0:8, :] = ext_sc[c:c + 8, :]
    y = y * jax.nn.sigmoid(y)

    row = lax.broadcasted_iota(jnp.int32, (c, c), 0)
    col = lax.broadcasted_iota(jnp.int32, (c, c), 1)
    incl = row >= col
    strict = row > col
    eye = (row == col).astype(F32)
    blk_masks = []
    bsz = GDN_INV_BLOCK
    while bsz <= c:
        blk_masks.append((row // bsz) == (col // bsz))
        bsz *= 2
    lower = incl.astype(BF16)
    upper = (row <= col).astype(BF16)

    ba = ba_ref[0]
    a_log = vec_ref[0:1, :]
    dt_bias = vec_ref[1:2, :]
    norm_w = vec_ref[2:3, :]
    beta_all = jax.nn.sigmoid(ba)
    g_col = -jnp.exp(a_log) * _softplus(ba + dt_bias)
    bat = bat_ref[...]
    g_row = -jnp.exp(cols_ref[0]) * _softplus(bat + cols_ref[1])
    if valid < c:
        ok_col = lax.broadcasted_iota(jnp.int32, ba.shape, 0) < valid
        beta_all = jnp.where(ok_col, beta_all, 0.0)
        g_col = jnp.where(ok_col, g_col, 0.0)
        g_row = jnp.where(lax.broadcasted_iota(jnp.int32, bat.shape, 1) < valid, g_row, 0.0)
    gc_col = sum(_dot(lower, p) for p in _split3(g_col))
    gc_row = sum(_dot(p, upper) for p in _split3(g_row))

    for h in range(heads):
        sl = slice(h * dk, (h + 1) * dk)
        qh = y[:, sl]
        kh = y[:, kdim + h * dk:kdim + (h + 1) * dk]
        vh = y[:, 2 * kdim + h * dk:2 * kdim + (h + 1) * dk]
        qh = qh * lax.rsqrt(jnp.sum(qh * qh, axis=1, keepdims=True) + NORM_EPS) * (dk ** -0.5)
        kh = kh * lax.rsqrt(jnp.sum(kh * kh, axis=1, keepdims=True) + NORM_EPS)
        beta = beta_all[:, h:h + 1]
        gcc = gc_col[:, heads + h:heads + h + 1]
        gcr = gc_row[heads + h:heads + h + 1, :]
        decay = jnp.where(incl, jnp.exp(jnp.where(incl, gcc - gcr, 0.0)), 0.0)
        kb = kh * beta
        vb = vh * beta
        egc = jnp.exp(gcc)
        kh_b = kh.astype(BF16)
        gram = _dot_nt(jnp.concatenate([kb, qh], axis=0).astype(BF16), kh_b)
        neg_m = jnp.where(strict, -(gram[:c] * decay), 0.0)
        qk = jnp.where(incl, gram[c:] * decay, 0.0)
        nb = jnp.where(blk_masks[0], neg_m, 0.0)
        t_inv = eye + nb
        pw = _dot3(nb, nb)
        span = 2
        while span < GDN_INV_BLOCK:
            last = span * 2 >= GDN_INV_BLOCK
            rhs = t_inv if last else jnp.concatenate([t_inv, pw], axis=1)
            prod = _dot3(pw, rhs)
            t_inv = t_inv + prod[:, :c]
            if not last:
                pw = prod[:, c:]
            span *= 2
        for lvl in range(1, len(blk_masks)):
            if valid <= GDN_INV_BLOCK << (lvl - 1):
                break
            off = jnp.where(jnp.logical_and(blk_masks[lvl], jnp.logical_not(blk_masks[lvl - 1])),
                            neg_m, 0.0).astype(BF16)
            t_b = t_inv.astype(BF16)
            t_inv = t_inv + _dot(t_b, _dot(off, t_b).astype(BF16))
        uw = _dot(t_inv.astype(BF16), jnp.concatenate([vb, kb * egc], axis=1).astype(BF16))
        s_old = s_ref[0, h]
        ws_qs = _dot(jnp.concatenate([uw[:, dv:], qh * egc], axis=0).astype(BF16), s_old.astype(BF16))
        v_new = uw[:, :dv] - ws_qs[:c]
        v_new_b = v_new.astype(BF16)
        o_h = ws_qs[c:] + _dot(qk.astype(BF16), v_new_b)
        g_last = gcc[c - 1:c, :]
        k_dec_t = kh.T * jnp.exp(g_last - gcr)
        s_ref[0, h] = s_old * jnp.exp(g_last) + _dot(k_dec_t.astype(BF16), v_new_b)
        gate = z_ref[0, :, sl]
        o_n = _rms(o_h, norm_w) * (gate * jax.nn.sigmoid(gate))
        o_ref[0, :, sl] = o_n.astype(o_ref.dtype)


def _gdn_core(qkv, z, ba, bat, buf8, s0, conv_w8, vec, cols, valid, conv_w):
    b, tc, cdim = qkv.shape
    _, heads, dk, dv = s0.shape
    c = GDN_CHUNK
    nc = tc // c
    return pl.pallas_call(
        functools.partial(_gdn_body, c=c, valid=valid, heads=heads, dk=dk, conv_w=conv_w),
        out_shape=(jax.ShapeDtypeStruct((b, tc, heads * dv), BF16),
                   jax.ShapeDtypeStruct(s0.shape, F32)),
        grid=(b, nc),
        in_specs=[pl.BlockSpec((1, c, cdim), lambda bi, ci: (bi, ci, 0)),
                  pl.BlockSpec((1, c, heads * dv), lambda bi, ci: (bi, ci, 0)),
                  pl.BlockSpec((1, c, LANE), lambda bi, ci: (bi, ci, 0)),
                  pl.BlockSpec((LANE, c), lambda bi, ci: (0, bi * nc + ci)),
                  pl.BlockSpec((1, 8, cdim), lambda bi, ci: (bi, 0, 0)),
                  pl.BlockSpec((1, heads, dk, dv), lambda bi, ci: (bi, 0, 0, 0)),
                  _full(conv_w8.shape), _full(vec.shape), _full(cols.shape)],
        out_specs=(pl.BlockSpec((1, c, heads * dv), lambda bi, ci: (bi, ci, 0)),
                   pl.BlockSpec((1, heads, dk, dv), lambda bi, ci: (bi, 0, 0, 0))),
        scratch_shapes=[pltpu.VMEM((c + 8, cdim), F32)],
        compiler_params=_params("parallel", "arbitrary"),
        name="gdn_core",
    )(qkv, z, ba, bat, buf8, s0, conv_w8, vec, cols)


def _gdn_layer(xp, xs, bp, tp, bs, ts, mem_state, conv_state, g_mix,
               w_in, conv_w, a_log, dt_bias, norm_w, w_o):
    _, heads, dk, dv = mem_state.shape
    cdim = conv_state.shape[2]
    val = heads * dv
    kw = conv_w.shape[0]
    c = GDN_CHUNK
    d = w_in.shape[0]
    n_in = w_in.shape[1]
    w_pad = jnp.pad(w_in, ((0, 0), (0, cdim + val + LANE - n_in))).astype(BF16)
    w_ba_t = jnp.transpose(w_pad[:, cdim + val:])
    conv_w8 = jnp.pad(conv_w, ((0, 8 - kw), (0, 0)))
    lane_pad = lambda v, off: jnp.pad(v, (off, LANE - off - v.shape[0]))
    vec = jnp.stack([lane_pad(a_log, heads), lane_pad(dt_bias, heads), norm_w]
                    + [jnp.zeros((LANE,), F32)] * 5)
    cols = jnp.stack([jnp.broadcast_to(lane_pad(a_log, heads)[:, None], (LANE, c)),
                      jnp.broadcast_to(lane_pad(dt_bias, heads)[:, None], (LANE, c))])
    w_o_b = w_o.astype(BF16)

    def run(x, b, t, buf, s0):
        qkv, z, ba, bat = _gdn_proj(x, g_mix, w_pad, w_ba_t, cdim, val)
        qkv3 = qkv.reshape(b, t, cdim)
        xp_rows = jnp.concatenate([buf, qkv3], axis=1)
        new_buf = xp_rows[:, xp_rows.shape[1] - (kw - 1):]
        buf8 = jnp.pad(buf, ((0, 0), (8 - (kw - 1), 0), (0, 0)))
        if t % c == 0:
            valid = c
            z3, ba3 = z.reshape(b, t, val), ba.reshape(b, t, LANE)
        else:
            assert t < c
            valid = t
            padt = lambda a: jnp.pad(a, ((0, 0), (0, c - t), (0, 0)))
            qkv3, z3, ba3 = padt(qkv3), padt(z.reshape(b, t, val)), padt(ba.reshape(b, t, LANE))
            bat = jnp.pad(bat.reshape(LANE, b, t), ((0, 0), (0, 0), (0, c - t))).reshape(LANE, b * c)
        o, s_new = _gdn_core(qkv3, z3, ba3, bat, buf8, s0, conv_w8, vec, cols, valid, kw)
        o = o[:, :t].reshape(b * t, val)
        return _out_proj(x, o, w_o_b), new_buf, s_new

    yp, conv_p, mem_p = run(xp, bp, tp, jnp.zeros((bp, kw - 1, cdim), F32),
                            jnp.zeros((bp, heads, dk, dv), F32))
    ys, conv_s, mem_s = run(xs, bs, ts, conv_state, mem_state)
    return yp, ys, mem_p, conv_p, mem_s, conv_s


def kernel(x_prompt, x_sample, cache_mla_ckv, cache_mla_krope, cache_sb_k, cache_sb_v, state_gdn_mem, state_gdn_conv, page_table, norm_ffn, ffn_w_gate, ffn_w_up, ffn_w_down, norm_mix, norm_final, mla_w_in, mla_norm_q, mla_norm_kv, mla_w_uq, mla_w_uk, mla_w_uv, mla_w_o, sb_w_in, sb_w_o, gdn_w_in, gdn_conv_w, gdn_a_log, gdn_dt_bias, gdn_norm_w, gdn_w_o):
    bp, tp, d = x_prompt.shape
    bs, ts, _ = x_sample.shape
    depth = norm_mix.shape[0]
    xp = x_prompt.reshape(bp * tp, d)
    xs = x_sample.reshape(bs * ts, d)
    wg = ffn_w_gate.astype(BF16)
    wu = ffn_w_up.astype(BF16)
    wd = ffn_w_down.astype(BF16)
    outs = {k: [] for k in ("mla_p", "mla_s", "sb_p", "sb_s", "gdn_p", "gdn_s")}
    for layer in range(depth):
        kind, idx = layer % 3, layer // 3
        ffn0 = (norm_ffn[layer, 0], wg[layer, 0], wu[layer, 0], wd[layer, 0])
        ffn1 = (norm_ffn[layer, 1], wg[layer, 1], wu[layer, 1], wd[layer, 1])
        xp, xs = _ffn_half(xp, *ffn0), _ffn_half(xs, *ffn0)
        if kind == 0:
            xp, xs, a, b_, c_, d_ = _mla_layer(
                xp, xs, bp, tp, bs, ts, cache_mla_ckv, cache_mla_krope, idx, page_table, norm_mix[layer],
                mla_w_in[idx], mla_norm_q[idx], mla_norm_kv[idx], mla_w_uq[idx], mla_w_uk[idx],
                mla_w_uv[idx], mla_w_o[idx])
            outs["mla_p"].append((a, b_))
            outs["mla_s"].append((c_, d_))
        elif kind == 1:
            xp, xs, a, b_, c_, d_ = _sb_layer(
                xp, xs, bp, tp, bs, ts, cache_sb_k, cache_sb_v, idx, page_table, norm_mix[layer],
                sb_w_in[idx], sb_w_o[idx])
            outs["sb_p"].append((a, b_))
            outs["sb_s"].append((c_, d_))
        else:
            xp, xs, a, b_, c_, d_ = _gdn_layer(
                xp, xs, bp, tp, bs, ts, state_gdn_mem[idx], state_gdn_conv[idx], norm_mix[layer],
                gdn_w_in[idx], gdn_conv_w[idx], gdn_a_log[idx], gdn_dt_bias[idx], gdn_norm_w[idx],
                gdn_w_o[idx])
            outs["gdn_p"].append((a, b_))
            outs["gdn_s"].append((c_, d_))
        fin = norm_final if layer == depth - 1 else None
        xp, xs = _ffn_half(xp, *ffn1, final_g=fin), _ffn_half(xs, *ffn1, final_g=fin)

    def stacked(key, i):
        return jnp.stack([o[i] for o in outs[key]])

    return (xp.reshape(bp, tp, d), xs.reshape(bs, ts, d),
            stacked("mla_p", 0), stacked("mla_p", 1), stacked("sb_p", 0), stacked("sb_p", 1),
            stacked("gdn_p", 0), stacked("gdn_p", 1),
            stacked("mla_s", 0), stacked("mla_s", 1), stacked("sb_s", 0), stacked("sb_s", 1),
            stacked("gdn_s", 0), stacked("gdn_s", 1))
```

```python
import functools
import math

import jax
import jax.numpy as jnp
from jax import lax
from jax.experimental import pallas as pl
from jax.experimental.pallas import tpu as pltpu

F32 = jnp.float32
BF16 = jnp.bfloat16

NORM_EPS = 1e-6
ROPE_THETA = 10000.0
LANE = 128
VMEM_LIMIT = 56 * 1024 * 1024

MLA_D_NOPE = 128
MLA_D_ROPE = 64
GDN_CHUNK = 128
GDN_INV_BLOCK = 16
FFN_CHUNK = 256


def _params(*sem):
    return pltpu.CompilerParams(dimension_semantics=sem, vmem_limit_bytes=VMEM_LIMIT)


def _tile(m, pref=512):
    t = pref
    while m % t:
        t //= 2
    return t


def _rms(x, w):
    return x * lax.rsqrt(jnp.mean(x * x, axis=-1, keepdims=True) + NORM_EPS) * w


def _dot(a, b):
    return jnp.dot(a, b, preferred_element_type=F32)


def _dot_nt(a, b):
    return lax.dot_general(a, b, (((1,), (1,)), ((), ())), preferred_element_type=F32)


def _softplus(z):
    return jnp.maximum(z, 0.0) + jnp.log(1.0 + jnp.exp(-jnp.abs(z)))


def _split2(x):
    hi = x.astype(BF16)
    lo = (x - hi.astype(F32)).astype(BF16)
    return hi, lo


def _dot3(a, b):
    a_hi, a_lo = _split2(a)
    b_hi, b_lo = _split2(b)
    return _dot(a_hi, b_hi) + _dot(a_hi, b_lo) + _dot(a_lo, b_hi)


def _split3(x):
    hi = x.astype(BF16)
    r = x - hi.astype(F32)
    mid = r.astype(BF16)
    lo = (r - mid.astype(F32)).astype(BF16)
    return hi, mid, lo


def _full(shape):
    n = len(shape)
    return pl.BlockSpec(shape, lambda *_: (0,) * n)


def _ffn_body(x_ref, g_ref, wg_ref, wu_ref, wd_ref, *rest, n_chunks, final):
    if final:
        gf_ref, o_ref = rest
    else:
        (o_ref,) = rest
    x = x_ref[...]
    h = _rms(x, g_ref[...]).astype(BF16)
    acc = jnp.zeros_like(x)
    for c in range(n_chunks):
        sl = slice(c * FFN_CHUNK, (c + 1) * FFN_CHUNK)
        gate = _dot(h, wg_ref[:, sl])
        up = _dot(h, wu_ref[:, sl])
        a = (gate * jax.nn.sigmoid(gate) * up).astype(BF16)
        acc = acc + _dot(a, wd_ref[sl, :])
    y = x + 0.5 * acc
    if final:
        y = _rms(y, gf_ref[...])
    o_ref[...] = y


def _ffn_half(x, g, wg, wu, wd, final_g=None):
    m, d = x.shape
    f = wg.shape[1]
    tm = _tile(m)
    final = final_g is not None
    ins = [x, g.reshape(1, d), wg, wu, wd]
    specs = [pl.BlockSpec((tm, d), lambda i: (i, 0)), _full((1, d)),
             _full((d, f)), _full((d, f)), _full((f, d))]
    if final:
        ins.append(final_g.reshape(1, d))
        specs.append(_full((1, d)))
    return pl.pallas_call(
        functools.partial(_ffn_body, n_chunks=f // FFN_CHUNK, final=final),
        out_shape=jax.ShapeDtypeStruct((m, d), F32),
        grid=(m // tm,),
        in_specs=specs,
        out_specs=pl.BlockSpec((tm, d), lambda i: (i, 0)),
        compiler_params=_params("parallel"),
        name="ffn_half",
    )(*ins)


def _oproj_body(x_ref, o_ref, w_ref, y_ref):
    y_ref[...] = x_ref[...] + _dot(o_ref[...], w_ref[...])


def _out_proj(x, o, w):
    m, d = x.shape
    k = o.shape[1]
    tm = _tile(m)
    return pl.pallas_call(
        _oproj_body,
        out_shape=jax.ShapeDtypeStruct((m, d), F32),
        grid=(m // tm,),
        in_specs=[pl.BlockSpec((tm, d), lambda i: (i, 0)),
                  pl.BlockSpec((tm, k), lambda i: (i, 0)), _full((k, d))],
        out_specs=pl.BlockSpec((tm, d), lambda i: (i, 0)),
        compiler_params=_params("parallel"),
        name="out_proj",
    )(x, o, w)


def _rot_cols(w):
    half = w.shape[-1] // 2
    return jnp.concatenate([-w[..., half:], w[..., :half]], axis=-1)


def _rope_tables(pos):
    half = MLA_D_ROPE // 2
    inv = ROPE_THETA ** (-jnp.arange(half, dtype=F32) / half)
    ang = pos.astype(F32)[:, None] * inv[None, :]
    return jnp.tile(jnp.cos(ang), (1, 4)), jnp.tile(jnp.sin(ang), (1, 4))


def _mla_weights(w_in, w_uq, w_uk, w_uv, w_o):
    ql, kvl = w_uq.shape[0], w_uk.shape[0]
    heads = w_uq.shape[1]
    wq, wkv, wkr = w_in[:, :ql], w_in[:, ql:ql + kvl], w_in[:, ql + kvl:]
    wkr_rot = _rot_cols(wkr)
    w_in_ext = jnp.concatenate([wq, wkv, wkr, wkr, wkr_rot, wkr_rot], axis=1).astype(BF16)
    w_nope = w_uq[:, :, :MLA_D_NOPE].reshape(ql, heads * MLA_D_NOPE)
    w_rope = w_uq[:, :, MLA_D_NOPE:]
    w_uq_all = jnp.concatenate(
        [w_nope, w_rope.reshape(ql, heads * MLA_D_ROPE),
         _rot_cols(w_rope).reshape(ql, heads * MLA_D_ROPE)], axis=1).astype(BF16)
    w_uk_t = jnp.transpose(w_uk, (1, 2, 0)).astype(BF16)
    w_uv_h = jnp.transpose(w_uv, (1, 0, 2)).astype(BF16)
    return w_in_ext, w_uq_all, w_uk_t, w_uv_h, w_o.astype(BF16)


def _mla_proj_body(x_ref, gm_ref, win_ref, nq_ref, nkv_ref, wuq_ref, wuk_ref, cos_ref, sin_ref,
                   q_ref, ckv_ref, kr_ref, kcat_ref, *, heads, scale):
    ql = nq_ref.shape[1]
    kvl = nkv_ref.shape[1]
    h = _rms(x_ref[...], gm_ref[...]).astype(BF16)
    proj = _dot(h, win_ref[...])
    cq = _rms(proj[:, :ql], nq_ref[...]).astype(BF16)
    ckv = _rms(proj[:, ql:ql + kvl], nkv_ref[...])
    cos = cos_ref[...]
    sin = sin_ref[...]
    o = ql + kvl
    kr2 = proj[:, o:o + LANE] * cos + proj[:, o + LANE:o + 2 * LANE] * sin
    ckv_ref[...] = ckv
    kr_ref[...] = kr2[:, :MLA_D_ROPE]
    kcat_ref[:, :kvl] = ckv.astype(BF16)
    kcat_ref[:, kvl:] = kr2.astype(BF16)
    lane = lax.broadcasted_iota(jnp.int32, cos.shape, 1)
    first = lane < MLA_D_ROPE
    nope_cols = heads * MLA_D_NOPE
    rope_cols = heads * MLA_D_ROPE
    for j in range(heads // 2):
        c0 = nope_cols + j * LANE
        qr = _dot(cq, wuq_ref[:, c0:c0 + LANE])
        qrot = _dot(cq, wuq_ref[:, c0 + rope_cols:c0 + rope_cols + LANE])
        qr2 = (qr * cos + qrot * sin) * scale
        for hh in range(2):
            hd = 2 * j + hh
            qn = _dot(cq, wuq_ref[:, hd * MLA_D_NOPE:(hd + 1) * MLA_D_NOPE]).astype(BF16)
            qlat = _dot(qn, wuk_ref[hd]) * scale
            q_ref[hd, :, :kvl] = qlat.astype(BF16)
            keep = first if hh == 0 else jnp.logical_not(first)
            q_ref[hd, :, kvl:] = jnp.where(keep, qr2, 0.0).astype(BF16)


def _mla_proj(x, g_mix, w_in_ext, norm_q, norm_kv, w_uq_all, w_uk_t, cos, sin):
    m, d = x.shape
    heads, _, kvl = w_uk_t.shape
    ql = norm_q.shape[0]
    tm = _tile(m)
    scale = (MLA_D_NOPE + MLA_D_ROPE) ** -0.5
    qw = kvl + LANE
    row = lambda i: (i, 0)
    return pl.pallas_call(
        functools.partial(_mla_proj_body, heads=heads, scale=scale),
        out_shape=(jax.ShapeDtypeStruct((heads, m, qw), BF16),
                   jax.ShapeDtypeStruct((m, kvl), F32),
                   jax.ShapeDtypeStruct((m, MLA_D_ROPE), F32),
                   jax.ShapeDtypeStruct((m, qw), BF16)),
        grid=(m // tm,),
        in_specs=[pl.BlockSpec((tm, d), row), _full((1, d)), _full(w_in_ext.shape),
                  _full((1, ql)), _full((1, kvl)), _full(w_uq_all.shape), _full(w_uk_t.shape),
                  pl.BlockSpec((tm, LANE), row), pl.BlockSpec((tm, LANE), row)],
        out_specs=(pl.BlockSpec((heads, tm, qw), lambda i: (0, i, 0)),
                   pl.BlockSpec((tm, kvl), row), pl.BlockSpec((tm, MLA_D_ROPE), row),
                   pl.BlockSpec((tm, qw), row)),
        compiler_params=_params("parallel"),
        name="mla_proj",
    )(x, g_mix.reshape(1, d), w_in_ext, norm_q.reshape(1, ql), norm_kv.reshape(1, kvl),
      w_uq_all, w_uk_t, cos, sin)


def _softmax_update(s, v, m_ref, l_ref, acc_ref, rows):
    nb = s.shape[1] // LANE
    m_prev = m_ref[rows]
    m_new = jnp.maximum(m_prev, jnp.max(s, axis=1, keepdims=True))
    alpha = jnp.exp(m_prev - m_new)
    ps = [jnp.exp(s[:, c * LANE:(c + 1) * LANE] - m_new) for c in range(nb)]
    psum = ps[0]
    for c in range(1, nb):
        psum = psum + ps[c]
    l_ref[rows] = alpha * l_ref[rows] + jnp.sum(psum, axis=1, keepdims=True)
    p = ps[0].astype(BF16) if nb == 1 else jnp.concatenate([x.astype(BF16) for x in ps], axis=1)
    reps = acc_ref.shape[1] // LANE
    acc_ref[rows] = jnp.concatenate([alpha] * reps, axis=1) * acc_ref[rows] + _dot(p, v)
    m_ref[rows] = m_new


def _softmax_result(l_ref, acc_ref):
    inv = 1.0 / l_ref[...]
    return acc_ref[...] * jnp.concatenate([inv] * (acc_ref.shape[1] // LANE), axis=1)


def _softmax_init(m_ref, l_ref, acc_ref):
    m_ref[...] = jnp.full(m_ref.shape, -jnp.inf, F32)
    l_ref[...] = jnp.zeros(l_ref.shape, F32)
    acc_ref[...] = jnp.zeros(acc_ref.shape, F32)


def _mla_prompt_body(q_ref, k_ref, o_ref, m_sc, l_sc, acc_sc, *, tq, tk, kvl):
    i = pl.program_id(1)
    heads = q_ref.shape[0]
    _softmax_init(m_sc, l_sc, acc_sc)
    n_slabs = 2
    hs = heads // n_slabs

    def step(j, masked):
        k = k_ref[0, pl.ds(pl.multiple_of(j * tk, tk), tk), :]
        for sl in range(n_slabs):
            q = q_ref[sl * hs:(sl + 1) * hs].reshape(hs * tq, q_ref.shape[-1])
            s = _dot_nt(q, k)
            if masked:
                qpos = i * tq + (lax.broadcasted_iota(jnp.int32, s.shape, 0) & (tq - 1))
                kpos = j * tk + lax.broadcasted_iota(jnp.int32, s.shape, 1)
                s = jnp.where(kpos <= qpos, s, -1e30)
            _softmax_update(s, k[:, :kvl], m_sc, l_sc, acc_sc, pl.ds(sl * hs * tq, hs * tq))

    n_full = (i * tq + 1) // tk

    def loop_body(j, carry):
        step(j, False)
        return carry

    lax.fori_loop(0, n_full, loop_body, 0)
    step(n_full, True)
    o_ref[...] = _softmax_result(l_sc, acc_sc).astype(o_ref.dtype).reshape(o_ref.shape)


def _mla_prompt_attn(q, kcat, b, t, kvl):
    heads, m, qw = q.shape
    tq = 128
    tk = min(512, t)
    nq = t // tq
    q4 = q.reshape(heads, b, t, qw)
    k3 = kcat.reshape(b, t, qw)
    out = pl.pallas_call(
        functools.partial(_mla_prompt_body, tq=tq, tk=tk, kvl=kvl),
        out_shape=jax.ShapeDtypeStruct((heads, b, t, kvl), BF16),
        grid=(b, nq),
        in_specs=[pl.BlockSpec((heads, 1, tq, qw), lambda bi, i: (0, bi, i, 0)),
                  pl.BlockSpec((1, t, qw), lambda bi, i: (bi, 0, 0))],
        out_specs=pl.BlockSpec((heads, 1, tq, kvl), lambda bi, i: (0, bi, i, 0)),
        scratch_shapes=[pltpu.VMEM((heads * tq, LANE), F32), pltpu.VMEM((heads * tq, LANE), F32),
                        pltpu.VMEM((heads * tq, kvl), F32)],
        compiler_params=_params("parallel", "arbitrary"),
        name="mla_prompt_attn",
    )(q4, k3)
    return out.reshape(heads, m, kvl)


def _mla_decode_body(pt_ref, ql_ref, qr_ref, *refs, n_pg, t_new):
    del pt_ref
    ckv_refs = refs[:n_pg]
    krt_refs = refs[n_pg:2 * n_pg]
    nl_ref, nrt_ref, o_ref, m_sc, l_sc, acc_sc = refs[2 * n_pg:]
    j = pl.program_id(1)

    @pl.when(j == 0)
    def _():
        _softmax_init(m_sc, l_sc, acc_sc)

    ql = ql_ref[0]
    qr = qr_ref[0]
    rows = pl.ds(0, m_sc.shape[0])
    kv = jnp.concatenate([r[...].astype(BF16) for r in ckv_refs], axis=0)
    krt = jnp.concatenate([r[...].astype(BF16) for r in krt_refs], axis=1)
    _softmax_update(_dot_nt(ql, kv) + _dot(qr, krt), kv, m_sc, l_sc, acc_sc, rows)

    @pl.when(j == pl.num_programs(1) - 1)
    def _():
        kn = nl_ref[0]
        s = _dot_nt(ql, kn) + _dot(qr, nrt_ref[0])
        tq = lax.broadcasted_iota(jnp.int32, s.shape, 0) % t_new
        col = lax.broadcasted_iota(jnp.int32, s.shape, 1)
        s = jnp.where(col <= tq, s, -1e30)
        _softmax_update(s, kn, m_sc, l_sc, acc_sc, rows)
        o_ref[0] = _softmax_result(l_sc, acc_sc).astype(o_ref.dtype)


def _mla_decode_attn(q_lat, q_rope, cache_ckv, cache_krt, layer, page_table, new_lat, new_rope_t, t_new):
    b, rows, kvl = q_lat.shape
    n_pages = page_table.shape[1]
    page = cache_ckv.shape[2]
    n_pg = 32
    while n_pages % n_pg:
        n_pg //= 2

    def page_spec(shape, n):
        return pl.BlockSpec((None, None) + shape,
                            lambda bi, j, pt: (layer, pt[bi, j * n_pg + n], 0, 0))

    per_b = lambda bi, j, pt: (bi, 0, 0)
    grid_spec = pltpu.PrefetchScalarGridSpec(
        num_scalar_prefetch=1,
        grid=(b, n_pages // n_pg),
        in_specs=([pl.BlockSpec((1, rows, kvl), per_b), pl.BlockSpec((1, rows, MLA_D_ROPE), per_b)]
                  + [page_spec((page, kvl), n) for n in range(n_pg)]
                  + [page_spec((MLA_D_ROPE, page), n) for n in range(n_pg)]
                  + [pl.BlockSpec((1, page, kvl), per_b), pl.BlockSpec((1, MLA_D_ROPE, page), per_b)]),
        out_specs=pl.BlockSpec((1, rows, kvl), per_b),
        scratch_shapes=[pltpu.VMEM((rows, LANE), F32), pltpu.VMEM((rows, LANE), F32),
                        pltpu.VMEM((rows, kvl), F32)],
    )
    return pl.pallas_call(
        functools.partial(_mla_decode_body, n_pg=n_pg, t_new=t_new),
        out_shape=jax.ShapeDtypeStruct((b, rows, kvl), BF16),
        grid_spec=grid_spec,
        compiler_params=_params("parallel", "arbitrary"),
        name="mla_decode_attn",
    )(page_table, q_lat, q_rope, *([cache_ckv] * n_pg), *([cache_krt] * n_pg), new_lat, new_rope_t)


def _mla_out_body(x_ref, o_ref, wuv_ref, wo_ref, y_ref, *, heads):
    dv = wuv_ref.shape[2]
    acc = x_ref[...]
    for j in range(heads // 2):
        a = _dot(o_ref[2 * j], wuv_ref[2 * j]).astype(BF16)
        bb = _dot(o_ref[2 * j + 1], wuv_ref[2 * j + 1]).astype(BF16)
        pair = jnp.concatenate([a, bb], axis=1)
        acc = acc + _dot(pair, wo_ref[2 * j * dv:(2 * j + 2) * dv, :])
    y_ref[...] = acc


def _mla_out(x, o_lat, w_uv_h, w_o):
    m, d = x.shape
    heads, _, kvl = o_lat.shape
    tm = _tile(m)
    return pl.pallas_call(
        functools.partial(_mla_out_body, heads=heads),
        out_shape=jax.ShapeDtypeStruct((m, d), F32),
        grid=(m // tm,),
        in_specs=[pl.BlockSpec((tm, d), lambda i: (i, 0)),
                  pl.BlockSpec((heads, tm, kvl), lambda i: (0, i, 0)),
                  _full(w_uv_h.shape), _full(w_o.shape)],
        out_specs=pl.BlockSpec((tm, d), lambda i: (i, 0)),
        compiler_params=_params("parallel"),
        name="mla_out",
    )(x, o_lat, w_uv_h, w_o)


def _mla_layer(xp, xs, bp, tp, bs, ts, cache_ckv, cache_kr, layer, page_table, g_mix,
               w_in, norm_q, norm_kv, w_uq, w_uk, w_uv, w_o):
    w_in_ext, w_uq_all, w_uk_t, w_uv_h, w_o_b = _mla_weights(w_in, w_uq, w_uk, w_uv, w_o)
    heads, _, kvl = w_uk_t.shape
    page = cache_ckv.shape[2]
    past = page_table.shape[1] * page
    cos_p, sin_p = _rope_tables(jnp.tile(jnp.arange(tp, dtype=jnp.int32), bp))
    q_p, ckv_p, kr_p, kcat_p = _mla_proj(xp, g_mix, w_in_ext, norm_q, norm_kv, w_uq_all, w_uk_t, cos_p, sin_p)
    ol_p = _mla_prompt_attn(q_p, kcat_p, bp, tp, kvl)
    yp = _mla_out(xp, ol_p, w_uv_h, w_o_b)
    cos_s, sin_s = _rope_tables(jnp.tile(past + jnp.arange(ts, dtype=jnp.int32), bs))
    q_s, ckv_s, kr_s, kcat_s = _mla_proj(xs, g_mix, w_in_ext, norm_q, norm_kv, w_uq_all, w_uk_t, cos_s, sin_s)
    q_s = jnp.transpose(q_s.reshape(heads, bs, ts, kvl + LANE), (1, 0, 2, 3)).reshape(bs, heads * ts, kvl + LANE)
    q_lat = q_s[..., :kvl]
    q_rope = q_s[..., kvl:kvl + MLA_D_ROPE] + q_s[..., kvl + MLA_D_ROPE:]
    kcat_s = kcat_s.reshape(bs, ts, kvl + LANE)
    new_lat = jnp.pad(kcat_s[..., :kvl], ((0, 0), (0, page - ts), (0, 0)))
    new_rope_t = jnp.pad(jnp.swapaxes(kcat_s[..., kvl:kvl + MLA_D_ROPE], 1, 2),
                         ((0, 0), (0, 0), (0, page - ts)))
    cache_krt = jnp.swapaxes(cache_kr, 2, 3)
    ol_s = _mla_decode_attn(q_lat, q_rope, cache_ckv, cache_krt, layer, page_table, new_lat, new_rope_t, ts)
    ol_s = jnp.transpose(ol_s.reshape(bs, heads, ts, kvl), (1, 0, 2, 3)).reshape(heads, bs * ts, kvl)
    ys = _mla_out(xs, ol_s, w_uv_h, w_o_b)
    return (yp, ys, ckv_p.reshape(bp, tp, kvl), kr_p.reshape(bp, tp, MLA_D_ROPE),
            ckv_s.reshape(bs, ts, kvl), kr_s.reshape(bs, ts, MLA_D_ROPE))


def _sb_proj_body(x_ref, g_ref, w_ref, q_ref, k_ref, v_ref, kb_ref, vb_ref, *, scale):
    n = q_ref.shape[1]
    h = _rms(x_ref[...], g_ref[...]).astype(BF16)
    q_ref[...] = (_dot(h, w_ref[:, :n]) * scale).astype(BF16)
    k = _dot(h, w_ref[:, n:2 * n])
    k_ref[...] = k
    kb_ref[...] = k.astype(BF16)
    v = _dot(h, w_ref[:, 2 * n:])
    v_ref[...] = v
    vb_ref[...] = v.astype(BF16)


def _sb_proj(x, g_mix, w_in, scale):
    m, d = x.shape
    n = w_in.shape[1] // 3
    tm = _tile(m)
    row = lambda i: (i, 0)
    return pl.pallas_call(
        functools.partial(_sb_proj_body, scale=scale),
        out_shape=(jax.ShapeDtypeStruct((m, n), BF16), jax.ShapeDtypeStruct((m, n), F32),
                   jax.ShapeDtypeStruct((m, n), F32), jax.ShapeDtypeStruct((m, n), BF16),
                   jax.ShapeDtypeStruct((m, n), BF16)),
        grid=(m // tm,),
        in_specs=[pl.BlockSpec((tm, d), row), _full((1, d)), _full(w_in.shape)],
        out_specs=tuple(pl.BlockSpec((tm, n), row) for _ in range(5)),
        compiler_params=_params("parallel"),
        name="sb_proj",
    )(x, g_mix.reshape(1, d), w_in)


def _sb_terms(z):
    l = jnp.log(1.0 + jnp.exp(-jnp.abs(z)))
    return jnp.maximum(z, 0.0) + l, jnp.minimum(z, 0.0) - l


def _strict_later(n):
    r = lax.broadcasted_iota(jnp.int32, (n, n), 0)
    c = lax.broadcasted_iota(jnp.int32, (n, n), 1)
    return (r > c).astype(BF16)


def _later_and_total(n):
    r = lax.broadcasted_iota(jnp.int32, (n, n + LANE), 0)
    c = lax.broadcasted_iota(jnp.int32, (n, n + LANE), 1)
    return jnp.logical_or(r > c, c >= n).astype(BF16)


def _sb_prompt_body(q_ref, k_ref, v_ref, w_ref, o_ref, acc_sc, car_sc, *, tq, hd, n_pairs):
    i = pl.program_id(2)
    acc_sc[...] = jnp.zeros(acc_sc.shape, F32)
    car_sc[...] = jnp.zeros(car_sc.shape, F32)
    first = lax.broadcasted_iota(jnp.int32, (tq, LANE), 1) < hd
    heads = range(2 * n_pairs)

    def block(j, masked):
        start = pl.multiple_of(j * tq, tq)
        w = w_ref[...]
        if masked:
            causal = (lax.broadcasted_iota(jnp.int32, (tq, tq), 1)
                      < lax.broadcasted_iota(jnp.int32, (tq, tq), 0))
        zs, vs = [], []
        for p in range(n_pairs):
            lanes = slice(p * LANE, (p + 1) * LANE)
            q2 = q_ref[0, :, lanes]
            k = k_ref[0, pl.ds(start, tq), lanes]
            zero = jnp.zeros_like(q2)
            for n in range(2):
                qh = jnp.where(first if n == 0 else jnp.logical_not(first), q2, zero)
                zs.append(_dot_nt(qh, k))
                vs.append(v_ref[0, pl.ds(start, tq), lanes])
        sps, lbs = [], []
        for z in zs:
            sp, log_beta = _sb_terms(z)
            if masked:
                sp = jnp.where(causal, sp, 0.0)
            sps.append(sp)
            lbs.append(log_beta)
        css = []
        for sp in sps:
            hi, lo = _split2(sp)
            css.append(_dot(jnp.concatenate([hi, lo], axis=1), w))
        for idx in heads:
            car = car_sc[idx]
            parts = []
            for c in range(tq // LANE):
                cl = slice(c * LANE, (c + 1) * LANE)
                parts.append(jnp.exp(lbs[idx][:, cl] - css[idx][:, cl] - car))
            a = jnp.concatenate(parts, axis=1)
            if masked:
                a = jnp.where(causal, a, 0.0)
            acc_sc[idx] = acc_sc[idx] + _dot(a.astype(BF16), vs[idx])
            car_sc[idx] = car + jnp.sum(sps[idx], axis=1, keepdims=True)

    block(i, True)

    def loop_body(jj, carry):
        block(i - 1 - jj, False)
        return carry

    lax.fori_loop(0, i, loop_body, 0)
    for p in range(n_pairs):
        o_ref[0, :, p * LANE:(p + 1) * LANE] = jnp.where(
            first, acc_sc[2 * p], acc_sc[2 * p + 1]).astype(o_ref.dtype)


def _sb_prompt_attn(q, kb, vb, b, t, hd):
    m, n = q.shape
    tq = min(256, t)
    nq = t // tq
    n_pairs = 4
    gw = n_pairs * LANE
    out = pl.pallas_call(
        functools.partial(_sb_prompt_body, tq=tq, hd=hd, n_pairs=n_pairs),
        out_shape=jax.ShapeDtypeStruct((b, t, n), BF16),
        grid=(b, n // gw, nq),
        in_specs=[pl.BlockSpec((1, tq, gw), lambda bi, p, i: (bi, i, p)),
                  pl.BlockSpec((1, t, gw), lambda bi, p, i: (bi, 0, p)),
                  pl.BlockSpec((1, t, gw), lambda bi, p, i: (bi, 0, p)),
                  _full((2 * tq, tq))],
        out_specs=pl.BlockSpec((1, tq, gw), lambda bi, p, i: (bi, i, p)),
        scratch_shapes=[pltpu.VMEM((2 * n_pairs, tq, LANE), F32), pltpu.VMEM((2 * n_pairs, tq, LANE), F32)],
        compiler_params=_params("parallel", "parallel", "arbitrary"),
        name="sb_prompt_attn",
    )(q.reshape(b, t, n), kb.reshape(b, t, n), vb.reshape(b, t, n), jnp.tile(_strict_later(tq), (2, 1)))
    return out.reshape(m, n)


def _sb_decode_body(pt_ref, q_ref, *refs, n_pg, t_new):
    del pt_ref
    k_refs = refs[:n_pg]
    v_refs = refs[n_pg:2 * n_pg]
    kn_ref, vn_ref, w_ref, o_ref, acc_sc, car_sc = refs[2 * n_pg:]
    jj = pl.program_id(1)
    q = q_ref[0]
    w = w_ref[...]

    @pl.when(jj == 0)
    def _():
        sp, log_beta = _sb_terms(_dot(q, kn_ref[0]))
        causal = (lax.broadcasted_iota(jnp.int32, sp.shape, 1)
                  < lax.broadcasted_iota(jnp.int32, sp.shape, 0) % t_new)
        sp = jnp.where(causal, sp, 0.0)
        hi, lo = _split2(sp)
        cs = _dot(hi, w) + _dot(lo, w)
        a = jnp.where(causal, jnp.exp(log_beta - cs[:, :LANE]), 0.0)
        acc_sc[...] = _dot_nt(a.astype(BF16), vn_ref[0])
        car_sc[...] = cs[:, LANE:]

    kt = jnp.concatenate([r[...].astype(BF16) for r in k_refs], axis=1)
    vt = jnp.concatenate([r[...].astype(BF16) for r in v_refs], axis=1)
    sp, log_beta = _sb_terms(_dot(q, kt))
    hi, lo = _split2(sp)
    css = []
    for n in range(n_pg):
        cl = slice(n * LANE, (n + 1) * LANE)
        css.append(_dot(hi[:, cl], w) + _dot(lo[:, cl], w))
    carry = car_sc[...]
    parts = [None] * n_pg
    for n in reversed(range(n_pg)):
        cl = slice(n * LANE, (n + 1) * LANE)
        parts[n] = jnp.exp(log_beta[:, cl] - css[n][:, :LANE] - carry).astype(BF16)
        carry = carry + css[n][:, LANE:]
    car_sc[...] = carry
    acc_sc[...] = acc_sc[...] + _dot_nt(jnp.concatenate(parts, axis=1), vt)

    @pl.when(jj == pl.num_programs(1) - 1)
    def _():
        o_ref[0] = acc_sc[...]


def _sb_decode_attn(qbd, cache_kt, cache_vt, layer, page_table, k_new_t, v_new_t, t_new):
    b, cols, n = qbd.shape
    n_pages = page_table.shape[1]
    page = cache_kt.shape[3]
    n_pg = 8
    while n_pages % n_pg:
        n_pg //= 2

    def page_spec(i):
        return pl.BlockSpec((None, None, n, page),
                            lambda bi, j, pt: (layer, pt[bi, n_pages - (j + 1) * n_pg + i], 0, 0))

    per_b = lambda bi, j, pt: (bi, 0, 0)
    grid_spec = pltpu.PrefetchScalarGridSpec(
        num_scalar_prefetch=1,
        grid=(b, n_pages // n_pg),
        in_specs=([pl.BlockSpec((1, cols, n), per_b)]
                  + [page_spec(i) for i in range(n_pg)] + [page_spec(i) for i in range(n_pg)]
                  + [pl.BlockSpec((1, n, page), per_b), pl.BlockSpec((1, n, page), per_b),
                     pl.BlockSpec((page, page + LANE), lambda bi, j, pt: (0, 0))]),
        out_specs=pl.BlockSpec((1, cols, n), per_b),
        scratch_shapes=[pltpu.VMEM((cols, n), F32), pltpu.VMEM((cols, LANE), F32)],
    )
    return pl.pallas_call(
        functools.partial(_sb_decode_body, n_pg=n_pg, t_new=t_new),
        out_shape=jax.ShapeDtypeStruct((b, cols, n), F32),
        grid_spec=grid_spec,
        compiler_params=_params("parallel", "arbitrary"),
        name="sb_decode_attn",
    )(page_table, qbd, *([cache_kt] * n_pg), *([cache_vt] * n_pg), k_new_t, v_new_t,
      _later_and_total(page))


def _sb_layer(xp, xs, bp, tp, bs, ts, cache_k, cache_v, layer, page_table, g_mix, w_in, w_o):
    heads, hd = cache_k.shape[3], cache_k.shape[4]
    n = heads * hd
    page = cache_k.shape[2]
    scale = hd ** -0.5
    w_in_b = w_in.astype(BF16)
    w_o_b = w_o.astype(BF16)
    q_p, k_p, v_p, kb_p, vb_p = _sb_proj(xp, g_mix, w_in_b, scale)
    o_p = _sb_prompt_attn(q_p, kb_p, vb_p, bp, tp, hd)
    yp = _out_proj(xp, o_p, w_o_b)
    q_s, k_s, v_s, kb_s, vb_s = _sb_proj(xs, g_mix, w_in_b, scale)
    q4 = q_s.reshape(bs, ts, heads, hd)
    eye = jnp.eye(heads, dtype=BF16)
    qbd = q4[:, :, :, None, :] * eye[None, None, :, :, None]
    qbd = jnp.transpose(qbd, (0, 2, 1, 3, 4)).reshape(bs, heads * ts, n)
    to_t = lambda a: jnp.pad(jnp.swapaxes(a.reshape(bs, ts, n), 1, 2), ((0, 0), (0, 0), (0, page - ts)))
    to_view = lambda c: jnp.transpose(c, (0, 1, 3, 4, 2)).reshape(c.shape[0], c.shape[1], n, page)
    res = _sb_decode_attn(qbd, to_view(cache_k), to_view(cache_v), layer, page_table,
                          to_t(kb_s), to_t(vb_s), ts)
    res = res.reshape(bs, heads, ts, heads, hd)
    idx = jnp.arange(heads)
    o_s = res[:, idx, :, idx, :]
    o_s = jnp.transpose(o_s, (1, 2, 0, 3)).reshape(bs * ts, n).astype(BF16)
    ys = _out_proj(xs, o_s, w_o_b)
    return (yp, ys, k_p.reshape(bp, tp, heads, hd), v_p.reshape(bp, tp, heads, hd),
            k_s.reshape(bs, ts, heads, hd), v_s.reshape(bs, ts, heads, hd))


def _gdn_proj_body(x_ref, g_ref, w_ref, wbat_ref, qkv_ref, z_ref, ba_ref, bat_ref):
    nq = qkv_ref.shape[1]
    nz = z_ref.shape[1]
    h = _rms(x_ref[...], g_ref[...]).astype(BF16)
    for c0 in range(0, nq, 1024):
        qkv_ref[:, c0:c0 + 1024] = _dot(h, w_ref[:, c0:c0 + 1024])
    z_ref[...] = _dot(h, w_ref[:, nq:nq + nz])
    ba_ref[...] = _dot(h, w_ref[:, nq + nz:])
    bat_ref[...] = _dot_nt(wbat_ref[...], h)


def _gdn_proj(x, g_mix, w_pad, w_ba_t, nq, nz):
    m, d = x.shape
    tm = _tile(m)
    row = lambda i: (i, 0)
    return pl.pallas_call(
        _gdn_proj_body,
        out_shape=(jax.ShapeDtypeStruct((m, nq), F32), jax.ShapeDtypeStruct((m, nz), F32),
                   jax.ShapeDtypeStruct((m, LANE), F32), jax.ShapeDtypeStruct((LANE, m), F32)),
        grid=(m // tm,),
        in_specs=[pl.BlockSpec((tm, d), row), _full((1, d)), _full(w_pad.shape), _full(w_ba_t.shape)],
        out_specs=(pl.BlockSpec((tm, nq), row), pl.BlockSpec((tm, nz), row),
                   pl.BlockSpec((tm, LANE), row), pl.BlockSpec((LANE, tm), lambda i: (0, i))),
        compiler_params=_params("parallel"),
        name="gdn_proj",
    )(x, g_mix.reshape(1, d), w_pad, w_ba_t)


def _gdn_body(qkv_ref, z_ref, ba_ref, bat_ref, buf_ref, s0_ref, cw_ref, vec_ref, cols_ref,
              o_ref, s_ref, ext_sc, *, c, valid, heads, dk, conv_w):
    ci = pl.program_id(1)
    kdim = heads * dk
    dv = s_ref.shape[3]

    @pl.when(ci == 0)
    def _():
        ext_sc[0:8, :] = buf_ref[0]
        s_ref[...] = s0_ref[...]

    ext_sc[8:8 + c, :] = qkv_ref[0]
    y = None
    for i in range(conv_w):
        lo = 8 - (conv_w - 1) + i
        term = ext_sc[lo:lo + c, :] * cw_ref[i:i + 1, :]
        y = term if y is None else y + term
    ext_sc[0:8, :] = ext_sc[c:c + 8, :]
    y = y * jax.nn.sigmoid(y)

    row = lax.broadcasted_iota(jnp.int32, (c, c), 0)
    col = lax.broadcasted_iota(jnp.int32, (c, c), 1)
    incl = row >= col
    strict = row > col
    eye = (row == col).astype(F32)
    blk_masks = []
    bsz = GDN_INV_BLOCK
    while bsz <= c:
        blk_masks.append((row // bsz) == (col // bsz))
        bsz *= 2
    lower = incl.astype(BF16)
    upper = (row <= col).astype(BF16)

    ba = ba_ref[0]
    a_log = vec_ref[0:1, :]
    dt_bias = vec_ref[1:2, :]
    norm_w = vec_ref[2:3, :]
    beta_all = jax.nn.sigmoid(ba)
    g_col = -jnp.exp(a_log) * _softplus(ba + dt_bias)
    bat = bat_ref[...]
    g_row = -jnp.exp(cols_ref[0]) * _softplus(bat + cols_ref[1])
    if valid < c:
        ok_col = lax.broadcasted_iota(jnp.int32, ba.shape, 0) < valid
        beta_all = jnp.where(ok_col, beta_all, 0.0)
        g_col = jnp.where(ok_col, g_col, 0.0)
        g_row = jnp.where(lax.broadcasted_iota(jnp.int32, bat.shape, 1) < valid, g_row, 0.0)
    gc_col = sum(_dot(lower, p) for p in _split3(g_col))
    gc_row = sum(_dot(p, upper) for p in _split3(g_row))

    hs = range(heads)
    l2n = lambda x: x * lax.rsqrt(jnp.sum(x * x, axis=1, keepdims=True) + NORM_EPS)
    q = [l2n(y[:, h * dk:(h + 1) * dk]) * (dk ** -0.5) for h in hs]
    k = [l2n(y[:, kdim + h * dk:kdim + (h + 1) * dk]) for h in hs]
    v = [y[:, 2 * kdim + h * dv:2 * kdim + (h + 1) * dv] for h in hs]
    beta = [beta_all[:, h:h + 1] for h in hs]
    gcc = [gc_col[:, heads + h:heads + h + 1] for h in hs]
    gcr = [gc_row[heads + h:heads + h + 1, :] for h in hs]
    decay = [jnp.where(incl, jnp.exp(jnp.where(incl, gcc[h] - gcr[h], 0.0)), 0.0) for h in hs]
    kb = [k[h] * beta[h] for h in hs]
    egc = [jnp.exp(gcc[h]) for h in hs]
    gram = [_dot_nt(jnp.concatenate([kb[h], q[h]], axis=0).astype(BF16), k[h].astype(BF16)) for h in hs]
    neg_m = [jnp.where(strict, -(gram[h][:c] * decay[h]), 0.0) for h in hs]
    qk = [jnp.where(incl, gram[h][c:] * decay[h], 0.0).astype(BF16) for h in hs]
    nb = [jnp.where(blk_masks[0], neg_m[h], 0.0) for h in hs]
    t_inv = [eye + nb[h] for h in hs]
    pw = [_dot3(nb[h], nb[h]) for h in hs]
    span = 2
    while span < GDN_INV_BLOCK:
        last = span * 2 >= GDN_INV_BLOCK
        prod = [_dot3(pw[h], t_inv[h] if last else jnp.concatenate([t_inv[h], pw[h]], axis=1)) for h in hs]
        t_inv = [t_inv[h] + prod[h][:, :c] for h in hs]
        if not last:
            pw = [prod[h][:, c:] for h in hs]
        span *= 2
    for lvl in range(1, len(blk_masks)):
        if valid <= GDN_INV_BLOCK << (lvl - 1):
            break
        ring = jnp.logical_and(blk_masks[lvl], jnp.logical_not(blk_masks[lvl - 1]))
        t_b = [t_inv[h].astype(BF16) for h in hs]
        inner = [_dot(jnp.where(ring, neg_m[h], 0.0).astype(BF16), t_b[h]).astype(BF16) for h in hs]
        t_inv = [t_inv[h] + _dot(t_b[h], inner[h]) for h in hs]
    uw = [_dot(t_inv[h].astype(BF16),
               jnp.concatenate([v[h] * beta[h], kb[h] * egc[h]], axis=1).astype(BF16)) for h in hs]
    s_old = [s_ref[0, h] for h in hs]
    ws_qs = [_dot(jnp.concatenate([uw[h][:, dv:], q[h] * egc[h]], axis=0).astype(BF16),
                  s_old[h].astype(BF16)) for h in hs]
    v_new = [(uw[h][:, :dv] - ws_qs[h][:c]).astype(BF16) for h in hs]
    o = [ws_qs[h][c:] + _dot(qk[h], v_new[h]) for h in hs]
    g_last = [gcc[h][c - 1:c, :] for h in hs]
    k_dec_t = [(k[h].T * jnp.exp(g_last[h] - gcr[h])).astype(BF16) for h in hs]
    for h in hs:
        s_ref[0, h] = s_old[h] * jnp.exp(g_last[h]) + _dot(k_dec_t[h], v_new[h])
    for h in hs:
        gate = z_ref[0, :, h * dv:(h + 1) * dv]
        o_ref[0, :, h * dv:(h + 1) * dv] = (
            _rms(o[h], norm_w) * (gate * jax.nn.sigmoid(gate))).astype(o_ref.dtype)


def _gdn_core(qkv, z, ba, bat, buf8, s0, conv_w8, vec, cols, valid, conv_w):
    b, tc, cdim = qkv.shape
    _, heads, dk, dv = s0.shape
    c = GDN_CHUNK
    nc = tc // c
    return pl.pallas_call(
        functools.partial(_gdn_body, c=c, valid=valid, heads=heads, dk=dk, conv_w=conv_w),
        out_shape=(jax.ShapeDtypeStruct((b, tc, heads * dv), BF16),
                   jax.ShapeDtypeStruct(s0.shape, F32)),
        grid=(b, nc),
        in_specs=[pl.BlockSpec((1, c, cdim), lambda bi, ci: (bi, ci, 0)),
                  pl.BlockSpec((1, c, heads * dv), lambda bi, ci: (bi, ci, 0)),
                  pl.BlockSpec((1, c, LANE), lambda bi, ci: (bi, ci, 0)),
                  pl.BlockSpec((LANE, c), lambda bi, ci: (0, bi * nc + ci)),
                  pl.BlockSpec((1, 8, cdim), lambda bi, ci: (bi, 0, 0)),
                  pl.BlockSpec((1, heads, dk, dv), lambda bi, ci: (bi, 0, 0, 0)),
                  _full(conv_w8.shape), _full(vec.shape), _full(cols.shape)],
        out_specs=(pl.BlockSpec((1, c, heads * dv), lambda bi, ci: (bi, ci, 0)),
                   pl.BlockSpec((1, heads, dk, dv), lambda bi, ci: (bi, 0, 0, 0))),
        scratch_shapes=[pltpu.VMEM((c + 8, cdim), F32)],
        compiler_params=_params("parallel", "arbitrary"),
        name="gdn_core",
    )(qkv, z, ba, bat, buf8, s0, conv_w8, vec, cols)


def _gdn_layer(xp, xs, bp, tp, bs, ts, mem_state, conv_state, g_mix,
               w_in, conv_w, a_log, dt_bias, norm_w, w_o):
    _, heads, dk, dv = mem_state.shape
    cdim = conv_state.shape[2]
    val = heads * dv
    kw = conv_w.shape[0]
    c = GDN_CHUNK
    d = w_in.shape[0]
    n_in = w_in.shape[1]
    w_pad = jnp.pad(w_in, ((0, 0), (0, cdim + val + LANE - n_in))).astype(BF16)
    w_ba_t = jnp.transpose(w_pad[:, cdim + val:])
    conv_w8 = jnp.pad(conv_w, ((0, 8 - kw), (0, 0)))
    lane_pad = lambda v, off: jnp.pad(v, (off, LANE - off - v.shape[0]))
    vec = jnp.stack([lane_pad(a_log, heads), lane_pad(dt_bias, heads), norm_w]
                    + [jnp.zeros((LANE,), F32)] * 5)
    cols = jnp.stack([jnp.broadcast_to(lane_pad(a_log, heads)[:, None], (LANE, c)),
                      jnp.broadcast_to(lane_pad(dt_bias, heads)[:, None], (LANE, c))])
    w_o_b = w_o.astype(BF16)

    def run(x, b, t, buf, s0):
        qkv, z, ba, bat = _gdn_proj(x, g_mix, w_pad, w_ba_t, cdim, val)
        qkv3 = qkv.reshape(b, t, cdim)
        xp_rows = jnp.concatenate([buf, qkv3], axis=1)
        new_buf = xp_rows[:, xp_rows.shape[1] - (kw - 1):]
        buf8 = jnp.pad(buf, ((0, 0), (8 - (kw - 1), 0), (0, 0)))
        if t % c == 0:
            valid = c
            z3, ba3 = z.reshape(b, t, val), ba.reshape(b, t, LANE)
        else:
            assert t < c
            valid = t
            padt = lambda a: jnp.pad(a, ((0, 0), (0, c - t), (0, 0)))
            qkv3, z3, ba3 = padt(qkv3), padt(z.reshape(b, t, val)), padt(ba.reshape(b, t, LANE))
            bat = jnp.pad(bat.reshape(LANE, b, t), ((0, 0), (0, 0), (0, c - t))).reshape(LANE, b * c)
        o, s_new = _gdn_core(qkv3, z3, ba3, bat, buf8, s0, conv_w8, vec, cols, valid, kw)
        o = o[:, :t].reshape(b * t, val)
        return _out_proj(x, o, w_o_b), new_buf, s_new

    yp, conv_p, mem_p = run(xp, bp, tp, jnp.zeros((bp, kw - 1, cdim), F32),
                            jnp.zeros((bp, heads, dk, dv), F32))
    ys, conv_s, mem_s = run(xs, bs, ts, conv_state, mem_state)
    return yp, ys, mem_p, conv_p, mem_s, conv_s


def kernel(x_prompt, x_sample, cache_mla_ckv, cache_mla_krope, cache_sb_k, cache_sb_v, state_gdn_mem, state_gdn_conv, page_table, norm_ffn, ffn_w_gate, ffn_w_up, ffn_w_down, norm_mix, norm_final, mla_w_in, mla_norm_q, mla_norm_kv, mla_w_uq, mla_w_uk, mla_w_uv, mla_w_o, sb_w_in, sb_w_o, gdn_w_in, gdn_conv_w, gdn_a_log, gdn_dt_bias, gdn_norm_w, gdn_w_o):
    bp, tp, d = x_prompt.shape
    bs, ts, _ = x_sample.shape
    depth = norm_mix.shape[0]
    xp = x_prompt.reshape(bp * tp, d)
    xs = x_sample.reshape(bs * ts, d)
    wg = ffn_w_gate.astype(BF16)
    wu = ffn_w_up.astype(BF16)
    wd = ffn_w_down.astype(BF16)
    outs = {k: [] for k in ("mla_p", "mla_s", "sb_p", "sb_s", "gdn_p", "gdn_s")}
    for layer in range(depth):
        kind, idx = layer % 3, layer // 3
        ffn0 = (norm_ffn[layer, 0], wg[layer, 0], wu[layer, 0], wd[layer, 0])
        ffn1 = (norm_ffn[layer, 1], wg[layer, 1], wu[layer, 1], wd[layer, 1])
        xp, xs = _ffn_half(xp, *ffn0), _ffn_half(xs, *ffn0)
        if kind == 0:
            xp, xs, a, b_, c_, d_ = _mla_layer(
                xp, xs, bp, tp, bs, ts, cache_mla_ckv, cache_mla_krope, idx, page_table, norm_mix[layer],
                mla_w_in[idx], mla_norm_q[idx], mla_norm_kv[idx], mla_w_uq[idx], mla_w_uk[idx],
                mla_w_uv[idx], mla_w_o[idx])
            outs["mla_p"].append((a, b_))
            outs["mla_s"].append((c_, d_))
        elif kind == 1:
            xp, xs, a, b_, c_, d_ = _sb_layer(
                xp, xs, bp, tp, bs, ts, cache_sb_k, cache_sb_v, idx, page_table, norm_mix[layer],
                sb_w_in[idx], sb_w_o[idx])
            outs["sb_p"].append((a, b_))
            outs["sb_s"].append((c_, d_))
        else:
            xp, xs, a, b_, c_, d_ = _gdn_layer(
                xp, xs, bp, tp, bs, ts, state_gdn_mem[idx], state_gdn_conv[idx], norm_mix[layer],
                gdn_w_in[idx], gdn_conv_w[idx], gdn_a_log[idx], gdn_dt_bias[idx], gdn_norm_w[idx],
                gdn_w_o[idx])
            outs["gdn_p"].append((a, b_))
            outs["gdn_s"].append((c_, d_))
        fin = norm_final if layer == depth - 1 else None
        xp, xs = _ffn_half(xp, *ffn1, final_g=fin), _ffn_half(xs, *ffn1, final_g=fin)

    def stacked(key, i):
        return jnp.stack([o[i] for o in outs[key]])

    return (xp.reshape(bp, tp, d), xs.reshape(bs, ts, d),
            stacked("mla_p", 0), stacked("mla_p", 1), stacked("sb_p", 0), stacked("sb_p", 1),
            stacked("gdn_p", 0), stacked("gdn_p", 1),
            stacked("mla_s", 0), stacked("mla_s", 1), stacked("sb_s", 0), stacked("sb_s", 1),
            stacked("gdn_s", 0), stacked("gdn_s", 1))
```

```python
import functools
import math

import jax
import jax.numpy as jnp
from jax import lax
from jax.experimental import pallas as pl
from jax.experimental.pallas import tpu as pltpu

F32 = jnp.float32
BF16 = jnp.bfloat16

NORM_EPS = 1e-6
ROPE_THETA = 10000.0
LANE = 128
VMEM_LIMIT = 56 * 1024 * 1024

MLA_D_NOPE = 128
MLA_D_ROPE = 64
GDN_CHUNK = 128
GDN_INV_BLOCK = 16
FFN_CHUNK = 256


def _params(*sem):
    return pltpu.CompilerParams(dimension_semantics=sem, vmem_limit_bytes=VMEM_LIMIT)


def _tile(m, pref=512):
    t = pref
    while m % t:
        t //= 2
    return t


def _rms(x, w):
    return x * lax.rsqrt(jnp.mean(x * x, axis=-1, keepdims=True) + NORM_EPS) * w


def _dot(a, b):
    return jnp.dot(a, b, preferred_element_type=F32)


def _dot_nt(a, b):
    return lax.dot_general(a, b, (((1,), (1,)), ((), ())), preferred_element_type=F32)


def _softplus(z):
    return jnp.maximum(z, 0.0) + jnp.log(1.0 + jnp.exp(-jnp.abs(z)))


def _split2(x):
    hi = x.astype(BF16)
    lo = (x - hi.astype(F32)).astype(BF16)
    return hi, lo


def _dot3(a, b):
    a_hi, a_lo = _split2(a)
    b_hi, b_lo = _split2(b)
    return _dot(a_hi, b_hi) + _dot(a_hi, b_lo) + _dot(a_lo, b_hi)


def _split3(x):
    hi = x.astype(BF16)
    r = x - hi.astype(F32)
    mid = r.astype(BF16)
    lo = (r - mid.astype(F32)).astype(BF16)
    return hi, mid, lo


def _full(shape, single=False):
    n = len(shape)
    if single:
        return pl.BlockSpec(shape, lambda *_: (0,) * n, pipeline_mode=pl.Buffered(1))
    return pl.BlockSpec(shape, lambda *_: (0,) * n)


def _ffn_body(x_ref, g_ref, wg_ref, wu_ref, wd_ref, *rest, n_chunks, final):
    if final:
        gf_ref, o_ref = rest
    else:
        (o_ref,) = rest
    x = x_ref[...]
    h = _rms(x, g_ref[...]).astype(BF16)
    acc = jnp.zeros_like(x)
    for c in range(n_chunks):
        sl = slice(c * FFN_CHUNK, (c + 1) * FFN_CHUNK)
        gate = _dot(h, wg_ref[:, sl])
        up = _dot(h, wu_ref[:, sl])
        a = (gate * jax.nn.sigmoid(gate) * up).astype(BF16)
        acc = acc + _dot(a, wd_ref[sl, :])
    y = x + 0.5 * acc
    if final:
        y = _rms(y, gf_ref[...])
    o_ref[...] = y


def _ffn_half(x, g, wg, wu, wd, final_g=None):
    m, d = x.shape
    f = wg.shape[1]
    tm = _tile(m, 1024)
    final = final_g is not None
    ins = [x, g.reshape(1, d), wg, wu, wd]
    specs = [pl.BlockSpec((tm, d), lambda i: (i, 0)), _full((1, d)),
             _full((d, f), True), _full((d, f), True), _full((f, d), True)]
    if final:
        ins.append(final_g.reshape(1, d))
        specs.append(_full((1, d)))
    return pl.pallas_call(
        functools.partial(_ffn_body, n_chunks=f // FFN_CHUNK, final=final),
        out_shape=jax.ShapeDtypeStruct((m, d), F32),
        grid=(m // tm,),
        in_specs=specs,
        out_specs=pl.BlockSpec((tm, d), lambda i: (i, 0)),
        compiler_params=_params("parallel"),
        name="ffn_half",
    )(*ins)


def _oproj_body(x_ref, o_ref, w_ref, y_ref):
    y_ref[...] = x_ref[...] + _dot(o_ref[...], w_ref[...])


def _out_proj(x, o, w):
    m, d = x.shape
    k = o.shape[1]
    tm = _tile(m)
    return pl.pallas_call(
        _oproj_body,
        out_shape=jax.ShapeDtypeStruct((m, d), F32),
        grid=(m // tm,),
        in_specs=[pl.BlockSpec((tm, d), lambda i: (i, 0)),
                  pl.BlockSpec((tm, k), lambda i: (i, 0)), _full((k, d))],
        out_specs=pl.BlockSpec((tm, d), lambda i: (i, 0)),
        compiler_params=_params("parallel"),
        name="out_proj",
    )(x, o, w)


def _rot_cols(w):
    half = w.shape[-1] // 2
    return jnp.concatenate([-w[..., half:], w[..., :half]], axis=-1)


def _rope_tables(pos):
    half = MLA_D_ROPE // 2
    inv = ROPE_THETA ** (-jnp.arange(half, dtype=F32) / half)
    ang = pos.astype(F32)[:, None] * inv[None, :]
    return jnp.tile(jnp.cos(ang), (1, 4)), jnp.tile(jnp.sin(ang), (1, 4))


def _mla_weights(w_in, w_uq, w_uk, w_uv, w_o):
    ql, kvl = w_uq.shape[0], w_uk.shape[0]
    heads = w_uq.shape[1]
    wq, wkv, wkr = w_in[:, :ql], w_in[:, ql:ql + kvl], w_in[:, ql + kvl:]
    wkr_rot = _rot_cols(wkr)
    w_in_ext = jnp.concatenate([wq, wkv, wkr, wkr, wkr_rot, wkr_rot], axis=1).astype(BF16)
    w_nope = w_uq[:, :, :MLA_D_NOPE].reshape(ql, heads * MLA_D_NOPE)
    w_rope = w_uq[:, :, MLA_D_NOPE:]
    w_uq_all = jnp.concatenate(
        [w_nope, w_rope.reshape(ql, heads * MLA_D_ROPE),
         _rot_cols(w_rope).reshape(ql, heads * MLA_D_ROPE)], axis=1).astype(BF16)
    w_uk_t = jnp.transpose(w_uk, (1, 2, 0)).astype(BF16)
    w_uv_h = jnp.transpose(w_uv, (1, 0, 2)).astype(BF16)
    return w_in_ext, w_uq_all, w_uk_t, w_uv_h, w_o.astype(BF16)


def _mla_proj_body(x_ref, gm_ref, win_ref, nq_ref, nkv_ref, wuq_ref, wuk_ref, cos_ref, sin_ref,
                   q_ref, ckv_ref, kr_ref, kcat_ref, *, heads, scale):
    ql = nq_ref.shape[1]
    kvl = nkv_ref.shape[1]
    h = _rms(x_ref[...], gm_ref[...]).astype(BF16)
    proj = _dot(h, win_ref[...])
    cq = _rms(proj[:, :ql], nq_ref[...]).astype(BF16)
    ckv = _rms(proj[:, ql:ql + kvl], nkv_ref[...])
    cos = cos_ref[...]
    sin = sin_ref[...]
    o = ql + kvl
    kr2 = proj[:, o:o + LANE] * cos + proj[:, o + LANE:o + 2 * LANE] * sin
    ckv_ref[...] = ckv
    kr_ref[...] = kr2[:, :MLA_D_ROPE]
    kcat_ref[:, :kvl] = ckv.astype(BF16)
    kcat_ref[:, kvl:] = kr2.astype(BF16)
    lane = lax.broadcasted_iota(jnp.int32, cos.shape, 1)
    first = lane < MLA_D_ROPE
    nope_cols = heads * MLA_D_NOPE
    rope_cols = heads * MLA_D_ROPE
    for j in range(heads // 2):
        c0 = nope_cols + j * LANE
        qr = _dot(cq, wuq_ref[:, c0:c0 + LANE])
        qrot = _dot(cq, wuq_ref[:, c0 + rope_cols:c0 + rope_cols + LANE])
        qr2 = (qr * cos + qrot * sin) * scale
        for hh in range(2):
            hd = 2 * j + hh
            qn = _dot(cq, wuq_ref[:, hd * MLA_D_NOPE:(hd + 1) * MLA_D_NOPE]).astype(BF16)
            qlat = _dot(qn, wuk_ref[hd]) * scale
            q_ref[hd, :, :kvl] = qlat.astype(BF16)
            keep = first if hh == 0 else jnp.logical_not(first)
            q_ref[hd, :, kvl:] = jnp.where(keep, qr2, 0.0).astype(BF16)


def _mla_proj(x, g_mix, w_in_ext, norm_q, norm_kv, w_uq_all, w_uk_t, cos, sin):
    m, d = x.shape
    heads, _, kvl = w_uk_t.shape
    ql = norm_q.shape[0]
    tm = _tile(m)
    scale = (MLA_D_NOPE + MLA_D_ROPE) ** -0.5
    qw = kvl + LANE
    row = lambda i: (i, 0)
    return pl.pallas_call(
        functools.partial(_mla_proj_body, heads=heads, scale=scale),
        out_shape=(jax.ShapeDtypeStruct((heads, m, qw), BF16),
                   jax.ShapeDtypeStruct((m, kvl), F32),
                   jax.ShapeDtypeStruct((m, MLA_D_ROPE), F32),
                   jax.ShapeDtypeStruct((m, qw), BF16)),
        grid=(m // tm,),
        in_specs=[pl.BlockSpec((tm, d), row), _full((1, d)), _full(w_in_ext.shape),
                  _full((1, ql)), _full((1, kvl)), _full(w_uq_all.shape), _full(w_uk_t.shape),
                  pl.BlockSpec((tm, LANE), row), pl.BlockSpec((tm, LANE), row)],
        out_specs=(pl.BlockSpec((heads, tm, qw), lambda i: (0, i, 0)),
                   pl.BlockSpec((tm, kvl), row), pl.BlockSpec((tm, MLA_D_ROPE), row),
                   pl.BlockSpec((tm, qw), row)),
        compiler_params=_params("parallel"),
        name="mla_proj",
    )(x, g_mix.reshape(1, d), w_in_ext, norm_q.reshape(1, ql), norm_kv.reshape(1, kvl),
      w_uq_all, w_uk_t, cos, sin)


def _softmax_update(s, v, m_ref, l_ref, acc_ref, rows):
    nb = s.shape[1] // LANE
    m_prev = m_ref[rows]
    m_new = jnp.maximum(m_prev, jnp.max(s, axis=1, keepdims=True))
    alpha = jnp.exp(m_prev - m_new)
    ps = [jnp.exp(s[:, c * LANE:(c + 1) * LANE] - m_new) for c in range(nb)]
    psum = ps[0]
    for c in range(1, nb):
        psum = psum + ps[c]
    l_ref[rows] = alpha * l_ref[rows] + jnp.sum(psum, axis=1, keepdims=True)
    p = ps[0].astype(BF16) if nb == 1 else jnp.concatenate([x.astype(BF16) for x in ps], axis=1)
    reps = acc_ref.shape[1] // LANE
    acc_ref[rows] = jnp.concatenate([alpha] * reps, axis=1) * acc_ref[rows] + _dot(p, v)
    m_ref[rows] = m_new


def _softmax_result(l_ref, acc_ref):
    inv = 1.0 / l_ref[...]
    return acc_ref[...] * jnp.concatenate([inv] * (acc_ref.shape[1] // LANE), axis=1)


def _softmax_init(m_ref, l_ref, acc_ref):
    m_ref[...] = jnp.full(m_ref.shape, -jnp.inf, F32)
    l_ref[...] = jnp.zeros(l_ref.shape, F32)
    acc_ref[...] = jnp.zeros(acc_ref.shape, F32)


def _mla_prompt_body(q_ref, k_ref, o_ref, m_sc, l_sc, acc_sc, *, tq, tk, kvl):
    i = pl.program_id(1)
    heads = q_ref.shape[0]
    _softmax_init(m_sc, l_sc, acc_sc)
    n_slabs = 2
    hs = heads // n_slabs

    def step(j, width, masked):
        k = k_ref[0, pl.ds(pl.multiple_of(j * tk, tk), width), :]
        for sl in range(n_slabs):
            q = q_ref[sl * hs:(sl + 1) * hs].reshape(hs * tq, q_ref.shape[-1])
            s = _dot_nt(q, k)
            if masked:
                qpos = i * tq + (lax.broadcasted_iota(jnp.int32, s.shape, 0) & (tq - 1))
                kpos = j * tk + lax.broadcasted_iota(jnp.int32, s.shape, 1)
                s = jnp.where(kpos <= qpos, s, -1e30)
            _softmax_update(s, k[:, :kvl], m_sc, l_sc, acc_sc, pl.ds(sl * hs * tq, hs * tq))

    per = tk // tq
    n_full = i // per

    def loop_body(j, carry):
        step(j, tk, False)
        return carry

    lax.fori_loop(0, n_full, loop_body, 0)
    for r in range(per):
        @pl.when(i % per == r)
        def _(r=r):
            step(n_full, (r + 1) * tq, True)

    o_ref[...] = _softmax_result(l_sc, acc_sc).astype(o_ref.dtype).reshape(o_ref.shape)


def _mla_prompt_attn(q, kcat, b, t, kvl):
    heads, m, qw = q.shape
    tq = 128
    tk = min(512, t)
    nq = t // tq
    q4 = q.reshape(heads, b, t, qw)
    k3 = kcat.reshape(b, t, qw)
    out = pl.pallas_call(
        functools.partial(_mla_prompt_body, tq=tq, tk=tk, kvl=kvl),
        out_shape=jax.ShapeDtypeStruct((heads, b, t, kvl), BF16),
        grid=(b, nq),
        in_specs=[pl.BlockSpec((heads, 1, tq, qw), lambda bi, i: (0, bi, i, 0)),
                  pl.BlockSpec((1, t, qw), lambda bi, i: (bi, 0, 0))],
        out_specs=pl.BlockSpec((heads, 1, tq, kvl), lambda bi, i: (0, bi, i, 0)),
        scratch_shapes=[pltpu.VMEM((heads * tq, LANE), F32), pltpu.VMEM((heads * tq, LANE), F32),
                        pltpu.VMEM((heads * tq, kvl), F32)],
        compiler_params=_params("parallel", "arbitrary"),
        name="mla_prompt_attn",
    )(q4, k3)
    return out.reshape(heads, m, kvl)


def _mla_decode_body(pt_ref, ql_ref, qr_ref, *refs, n_pg, t_new):
    del pt_ref
    ckv_refs = refs[:n_pg]
    krt_refs = refs[n_pg:2 * n_pg]
    nl_ref, nrt_ref, o_ref, m_sc, l_sc, acc_sc = refs[2 * n_pg:]
    j = pl.program_id(1)

    @pl.when(j == 0)
    def _():
        _softmax_init(m_sc, l_sc, acc_sc)

    ql = ql_ref[0]
    qr = qr_ref[0]
    rows = pl.ds(0, m_sc.shape[0])
    kv = jnp.concatenate([r[...].astype(BF16) for r in ckv_refs], axis=0)
    krt = jnp.concatenate([r[...].astype(BF16) for r in krt_refs], axis=1)
    _softmax_update(_dot_nt(ql, kv) + _dot(qr, krt), kv, m_sc, l_sc, acc_sc, rows)

    @pl.when(j == pl.num_programs(1) - 1)
    def _():
        kn = nl_ref[0]
        s = _dot_nt(ql, kn) + _dot(qr, nrt_ref[0])
        tq = lax.broadcasted_iota(jnp.int32, s.shape, 0) % t_new
        col = lax.broadcasted_iota(jnp.int32, s.shape, 1)
        s = jnp.where(col <= tq, s, -1e30)
        _softmax_update(s, kn, m_sc, l_sc, acc_sc, rows)
        o_ref[0] = _softmax_result(l_sc, acc_sc).astype(o_ref.dtype)


def _mla_decode_attn(q_lat, q_rope, cache_ckv, cache_krt, layer, page_table, new_lat, new_rope_t, t_new):
    b, rows, kvl = q_lat.shape
    n_pages = page_table.shape[1]
    page = cache_ckv.shape[2]
    n_pg = 32
    while n_pages % n_pg:
        n_pg //= 2

    def page_spec(shape, n):
        return pl.BlockSpec((None, None) + shape,
                            lambda bi, j, pt: (layer, pt[bi, j * n_pg + n], 0, 0))

    per_b = lambda bi, j, pt: (bi, 0, 0)
    grid_spec = pltpu.PrefetchScalarGridSpec(
        num_scalar_prefetch=1,
        grid=(b, n_pages // n_pg),
        in_specs=([pl.BlockSpec((1, rows, kvl), per_b), pl.BlockSpec((1, rows, MLA_D_ROPE), per_b)]
                  + [page_spec((page, kvl), n) for n in range(n_pg)]
                  + [page_spec((MLA_D_ROPE, page), n) for n in range(n_pg)]
                  + [pl.BlockSpec((1, page, kvl), per_b), pl.BlockSpec((1, MLA_D_ROPE, page), per_b)]),
        out_specs=pl.BlockSpec((1, rows, kvl), per_b),
        scratch_shapes=[pltpu.VMEM((rows, LANE), F32), pltpu.VMEM((rows, LANE), F32),
                        pltpu.VMEM((rows, kvl), F32)],
    )
    return pl.pallas_call(
        functools.partial(_mla_decode_body, n_pg=n_pg, t_new=t_new),
        out_shape=jax.ShapeDtypeStruct((b, rows, kvl), BF16),
        grid_spec=grid_spec,
        compiler_params=_params("parallel", "arbitrary"),
        name="mla_decode_attn",
    )(page_table, q_lat, q_rope, *([cache_ckv] * n_pg), *([cache_krt] * n_pg), new_lat, new_rope_t)


def _mla_out_body(x_ref, o_ref, wuv_ref, wo_ref, y_ref, *, heads):
    dv = wuv_ref.shape[2]
    acc = x_ref[...]
    for j in range(heads // 2):
        a = _dot(o_ref[2 * j], wuv_ref[2 * j]).astype(BF16)
        bb = _dot(o_ref[2 * j + 1], wuv_ref[2 * j + 1]).astype(BF16)
        pair = jnp.concatenate([a, bb], axis=1)
        acc = acc + _dot(pair, wo_ref[2 * j * dv:(2 * j + 2) * dv, :])
    y_ref[...] = acc


def _mla_out(x, o_lat, w_uv_h, w_o):
    m, d = x.shape
    heads, _, kvl = o_lat.shape
    tm = _tile(m)
    return pl.pallas_call(
        functools.partial(_mla_out_body, heads=heads),
        out_shape=jax.ShapeDtypeStruct((m, d), F32),
        grid=(m // tm,),
        in_specs=[pl.BlockSpec((tm, d), lambda i: (i, 0)),
                  pl.BlockSpec((heads, tm, kvl), lambda i: (0, i, 0)),
                  _full(w_uv_h.shape), _full(w_o.shape)],
        out_specs=pl.BlockSpec((tm, d), lambda i: (i, 0)),
        compiler_params=_params("parallel"),
        name="mla_out",
    )(x, o_lat, w_uv_h, w_o)


def _mla_layer(xp, xs, bp, tp, bs, ts, cache_ckv, cache_kr, layer, page_table, g_mix,
               w_in, norm_q, norm_kv, w_uq, w_uk, w_uv, w_o):
    w_in_ext, w_uq_all, w_uk_t, w_uv_h, w_o_b = _mla_weights(w_in, w_uq, w_uk, w_uv, w_o)
    heads, _, kvl = w_uk_t.shape
    page = cache_ckv.shape[2]
    past = page_table.shape[1] * page
    cos_p, sin_p = _rope_tables(jnp.tile(jnp.arange(tp, dtype=jnp.int32), bp))
    q_p, ckv_p, kr_p, kcat_p = _mla_proj(xp, g_mix, w_in_ext, norm_q, norm_kv, w_uq_all, w_uk_t, cos_p, sin_p)
    ol_p = _mla_prompt_attn(q_p, kcat_p, bp, tp, kvl)
    yp = _mla_out(xp, ol_p, w_uv_h, w_o_b)
    cos_s, sin_s = _rope_tables(jnp.tile(past + jnp.arange(ts, dtype=jnp.int32), bs))
    q_s, ckv_s, kr_s, kcat_s = _mla_proj(xs, g_mix, w_in_ext, norm_q, norm_kv, w_uq_all, w_uk_t, cos_s, sin_s)
    q_s = jnp.transpose(q_s.reshape(heads, bs, ts, kvl + LANE), (1, 0, 2, 3)).reshape(bs, heads * ts, kvl + LANE)
    q_lat = q_s[..., :kvl]
    q_rope = q_s[..., kvl:kvl + MLA_D_ROPE] + q_s[..., kvl + MLA_D_ROPE:]
    kcat_s = kcat_s.reshape(bs, ts, kvl + LANE)
    new_lat = jnp.pad(kcat_s[..., :kvl], ((0, 0), (0, page - ts), (0, 0)))
    new_rope_t = jnp.pad(jnp.swapaxes(kcat_s[..., kvl:kvl + MLA_D_ROPE], 1, 2),
                         ((0, 0), (0, 0), (0, page - ts)))
    cache_krt = jnp.swapaxes(cache_kr, 2, 3)
    ol_s = _mla_decode_attn(q_lat, q_rope, cache_ckv, cache_krt, layer, page_table, new_lat, new_rope_t, ts)
    ol_s = jnp.transpose(ol_s.reshape(bs, heads, ts, kvl), (1, 0, 2, 3)).reshape(heads, bs * ts, kvl)
    ys = _mla_out(xs, ol_s, w_uv_h, w_o_b)
    return (yp, ys, ckv_p.reshape(bp, tp, kvl), kr_p.reshape(bp, tp, MLA_D_ROPE),
            ckv_s.reshape(bs, ts, kvl), kr_s.reshape(bs, ts, MLA_D_ROPE))


def _sb_proj_body(x_ref, g_ref, w_ref, q_ref, k_ref, v_ref, kb_ref, vb_ref, *, scale):
    n = q_ref.shape[1]
    h = _rms(x_ref[...], g_ref[...]).astype(BF16)
    q_ref[...] = (_dot(h, w_ref[:, :n]) * scale).astype(BF16)
    k = _dot(h, w_ref[:, n:2 * n])
    k_ref[...] = k
    kb_ref[...] = k.astype(BF16)
    v = _dot(h, w_ref[:, 2 * n:])
    v_ref[...] = v
    vb_ref[...] = v.astype(BF16)


def _sb_proj(x, g_mix, w_in, scale):
    m, d = x.shape
    n = w_in.shape[1] // 3
    tm = _tile(m)
    row = lambda i: (i, 0)
    return pl.pallas_call(
        functools.partial(_sb_proj_body, scale=scale),
        out_shape=(jax.ShapeDtypeStruct((m, n), BF16), jax.ShapeDtypeStruct((m, n), F32),
                   jax.ShapeDtypeStruct((m, n), F32), jax.ShapeDtypeStruct((m, n), BF16),
                   jax.ShapeDtypeStruct((m, n), BF16)),
        grid=(m // tm,),
        in_specs=[pl.BlockSpec((tm, d), row), _full((1, d)), _full(w_in.shape)],
        out_specs=tuple(pl.BlockSpec((tm, n), row) for _ in range(5)),
        compiler_params=_params("parallel"),
        name="sb_proj",
    )(x, g_mix.reshape(1, d), w_in)


def _sb_terms(z):
    l = jnp.log(1.0 + jnp.exp(-jnp.abs(z)))
    return jnp.maximum(z, 0.0) + l, jnp.minimum(z, 0.0) - l


def _strict_later(n):
    r = lax.broadcasted_iota(jnp.int32, (n, n), 0)
    c = lax.broadcasted_iota(jnp.int32, (n, n), 1)
    return (r > c).astype(BF16)


def _later_and_total(n):
    r = lax.broadcasted_iota(jnp.int32, (n, n + LANE), 0)
    c = lax.broadcasted_iota(jnp.int32, (n, n + LANE), 1)
    return jnp.logical_or(r > c, c >= n).astype(BF16)


def _sb_prompt_body(q_ref, k_ref, v_ref, w_ref, o_ref, acc_sc, car_sc, *, tq, hd, n_pairs):
    i = pl.program_id(2)
    acc_sc[...] = jnp.zeros(acc_sc.shape, F32)
    car_sc[...] = jnp.zeros(car_sc.shape, F32)
    first = lax.broadcasted_iota(jnp.int32, (tq, LANE), 1) < hd
    heads = range(2 * n_pairs)

    def block(j, masked):
        start = pl.multiple_of(j * tq, tq)
        w = w_ref[...]
        if masked:
            causal = (lax.broadcasted_iota(jnp.int32, (tq, tq), 1)
                      < lax.broadcasted_iota(jnp.int32, (tq, tq), 0))
        zs, vs = [], []
        for p in range(n_pairs):
            lanes = slice(p * LANE, (p + 1) * LANE)
            q2 = q_ref[0, :, lanes]
            k = k_ref[0, pl.ds(start, tq), lanes]
            zero = jnp.zeros_like(q2)
            for n in range(2):
                qh = jnp.where(first if n == 0 else jnp.logical_not(first), q2, zero)
                zs.append(_dot_nt(qh, k))
                vs.append(v_ref[0, pl.ds(start, tq), lanes])
        sps, lbs = [], []
        for z in zs:
            sp, log_beta = _sb_terms(z)
            if masked:
                sp = jnp.where(causal, sp, 0.0)
            sps.append(sp)
            lbs.append(log_beta)
        css = []
        for sp in sps:
            hi, lo = _split2(sp)
            css.append(_dot(jnp.concatenate([hi, lo], axis=1), w))
        for idx in heads:
            car = car_sc[idx]
            parts = []
            for c in range(tq // LANE):
                cl = slice(c * LANE, (c + 1) * LANE)
                parts.append(jnp.exp(lbs[idx][:, cl] - css[idx][:, cl] - car))
            a = jnp.concatenate(parts, axis=1)
            if masked:
                a = jnp.where(causal, a, 0.0)
            acc_sc[idx] = acc_sc[idx] + _dot(a.astype(BF16), vs[idx])
            car_sc[idx] = car + jnp.sum(sps[idx], axis=1, keepdims=True)

    block(i, True)

    def loop_body(jj, carry):
        block(i - 1 - jj, False)
        return carry

    lax.fori_loop(0, i, loop_body, 0)
    for p in range(n_pairs):
        o_ref[0, :, p * LANE:(p + 1) * LANE] = jnp.where(
            first, acc_sc[2 * p], acc_sc[2 * p + 1]).astype(o_ref.dtype)


def _sb_prompt_attn(q, kb, vb, b, t, hd):
    m, n = q.shape
    tq = min(256, t)
    nq = t // tq
    n_pairs = 4
    gw = n_pairs * LANE
    out = pl.pallas_call(
        functools.partial(_sb_prompt_body, tq=tq, hd=hd, n_pairs=n_pairs),
        out_shape=jax.ShapeDtypeStruct((b, t, n), BF16),
        grid=(b, n // gw, nq),
        in_specs=[pl.BlockSpec((1, tq, gw), lambda bi, p, i: (bi, i, p)),
                  pl.BlockSpec((1, t, gw), lambda bi, p, i: (bi, 0, p)),
                  pl.BlockSpec((1, t, gw), lambda bi, p, i: (bi, 0, p)),
                  _full((2 * tq, tq))],
        out_specs=pl.BlockSpec((1, tq, gw), lambda bi, p, i: (bi, i, p)),
        scratch_shapes=[pltpu.VMEM((2 * n_pairs, tq, LANE), F32), pltpu.VMEM((2 * n_pairs, tq, LANE), F32)],
        compiler_params=_params("parallel", "parallel", "arbitrary"),
        name="sb_prompt_attn",
    )(q.reshape(b, t, n), kb.reshape(b, t, n), vb.reshape(b, t, n), jnp.tile(_strict_later(tq), (2, 1)))
    return out.reshape(m, n)


def _sb_decode_body(pt_ref, q_ref, *refs, n_pg, t_new):
    del pt_ref
    k_refs = refs[:n_pg]
    v_refs = refs[n_pg:2 * n_pg]
    kn_ref, vn_ref, w_ref, o_ref, acc_sc, car_sc = refs[2 * n_pg:]
    jj = pl.program_id(1)
    q = q_ref[0]
    w = w_ref[...]

    @pl.when(jj == 0)
    def _():
        sp, log_beta = _sb_terms(_dot(q, kn_ref[0]))
        causal = (lax.broadcasted_iota(jnp.int32, sp.shape, 1)
                  < lax.broadcasted_iota(jnp.int32, sp.shape, 0) % t_new)
        sp = jnp.where(causal, sp, 0.0)
        hi, lo = _split2(sp)
        cs = _dot(hi, w) + _dot(lo, w)
        a = jnp.where(causal, jnp.exp(log_beta - cs[:, :LANE]), 0.0)
        acc_sc[...] = _dot_nt(a.astype(BF16), vn_ref[0])
        car_sc[...] = cs[:, LANE:]

    kt = jnp.concatenate([r[...].astype(BF16) for r in k_refs], axis=1)
    vt = jnp.concatenate([r[...].astype(BF16) for r in v_refs], axis=1)
    sp, log_beta = _sb_terms(_dot(q, kt))
    hi, lo = _split2(sp)
    css = []
    for n in range(n_pg):
        cl = slice(n * LANE, (n + 1) * LANE)
        css.append(_dot(hi[:, cl], w) + _dot(lo[:, cl], w))
    carry = car_sc[...]
    parts = [None] * n_pg
    for n in reversed(range(n_pg)):
        cl = slice(n * LANE, (n + 1) * LANE)
        parts[n] = jnp.exp(log_beta[:, cl] - css[n][:, :LANE] - carry).astype(BF16)
        carry = carry + css[n][:, LANE:]
    car_sc[...] = carry
    acc_sc[...] = acc_sc[...] + _dot_nt(jnp.concatenate(parts, axis=1), vt)

    @pl.when(jj == pl.num_programs(1) - 1)
    def _():
        o_ref[0] = acc_sc[...]


def _sb_decode_attn(qbd, cache_kt, cache_vt, layer, page_table, k_new_t, v_new_t, t_new):
    b, cols, n = qbd.shape
    n_pages = page_table.shape[1]
    page = cache_kt.shape[3]
    n_pg = 8
    while n_pages % n_pg:
        n_pg //= 2

    def page_spec(i):
        return pl.BlockSpec((None, None, n, page),
                            lambda bi, j, pt: (layer, pt[bi, n_pages - (j + 1) * n_pg + i], 0, 0))

    per_b = lambda bi, j, pt: (bi, 0, 0)
    grid_spec = pltpu.PrefetchScalarGridSpec(
        num_scalar_prefetch=1,
        grid=(b, n_pages // n_pg),
        in_specs=([pl.BlockSpec((1, cols, n), per_b)]
                  + [page_spec(i) for i in range(n_pg)] + [page_spec(i) for i in range(n_pg)]
                  + [pl.BlockSpec((1, n, page), per_b), pl.BlockSpec((1, n, page), per_b),
                     pl.BlockSpec((page, page + LANE), lambda bi, j, pt: (0, 0))]),
        out_specs=pl.BlockSpec((1, cols, n), per_b),
        scratch_shapes=[pltpu.VMEM((cols, n), F32), pltpu.VMEM((cols, LANE), F32)],
    )
    return pl.pallas_call(
        functools.partial(_sb_decode_body, n_pg=n_pg, t_new=t_new),
        out_shape=jax.ShapeDtypeStruct((b, cols, n), F32),
        grid_spec=grid_spec,
        compiler_params=_params("parallel", "arbitrary"),
        name="sb_decode_attn",
    )(page_table, qbd, *([cache_kt] * n_pg), *([cache_vt] * n_pg), k_new_t, v_new_t,
      _later_and_total(page))


def _sb_layer(xp, xs, bp, tp, bs, ts, cache_k, cache_v, layer, page_table, g_mix, w_in, w_o):
    heads, hd = cache_k.shape[3], cache_k.shape[4]
    n = heads * hd
    page = cache_k.shape[2]
    scale = hd ** -0.5
    w_in_b = w_in.astype(BF16)
    w_o_b = w_o.astype(BF16)
    q_p, k_p, v_p, kb_p, vb_p = _sb_proj(xp, g_mix, w_in_b, scale)
    o_p = _sb_prompt_attn(q_p, kb_p, vb_p, bp, tp, hd)
    yp = _out_proj(xp, o_p, w_o_b)
    q_s, k_s, v_s, kb_s, vb_s = _sb_proj(xs, g_mix, w_in_b, scale)
    q4 = q_s.reshape(bs, ts, heads, hd)
    eye = jnp.eye(heads, dtype=BF16)
    qbd = q4[:, :, :, None, :] * eye[None, None, :, :, None]
    qbd = jnp.transpose(qbd, (0, 2, 1, 3, 4)).reshape(bs, heads * ts, n)
    to_t = lambda a: jnp.pad(jnp.swapaxes(a.reshape(bs, ts, n), 1, 2), ((0, 0), (0, 0), (0, page - ts)))
    to_view = lambda c: jnp.transpose(c, (0, 1, 3, 4, 2)).reshape(c.shape[0], c.shape[1], n, page)
    res = _sb_decode_attn(qbd, to_view(cache_k), to_view(cache_v), layer, page_table,
                          to_t(kb_s), to_t(vb_s), ts)
    res = res.reshape(bs, heads, ts, heads, hd)
    idx = jnp.arange(heads)
    o_s = res[:, idx, :, idx, :]
    o_s = jnp.transpose(o_s, (1, 2, 0, 3)).reshape(bs * ts, n).astype(BF16)
    ys = _out_proj(xs, o_s, w_o_b)
    return (yp, ys, k_p.reshape(bp, tp, heads, hd), v_p.reshape(bp, tp, heads, hd),
            k_s.reshape(bs, ts, heads, hd), v_s.reshape(bs, ts, heads, hd))


def _gdn_proj_body(x_ref, g_ref, w_ref, wbat_ref, qkv_ref, z_ref, ba_ref, bat_ref):
    nq = qkv_ref.shape[1]
    nz = z_ref.shape[1]
    h = _rms(x_ref[...], g_ref[...]).astype(BF16)
    for c0 in range(0, nq, 1024):
        qkv_ref[:, c0:c0 + 1024] = _dot(h, w_ref[:, c0:c0 + 1024])
    z_ref[...] = _dot(h, w_ref[:, nq:nq + nz])
    ba_ref[...] = _dot(h, w_ref[:, nq + nz:])
    bat_ref[...] = _dot_nt(wbat_ref[...], h)


def _gdn_proj(x, g_mix, w_pad, w_ba_t, nq, nz):
    m, d = x.shape
    tm = _tile(m)
    row = lambda i: (i, 0)
    return pl.pallas_call(
        _gdn_proj_body,
        out_shape=(jax.ShapeDtypeStruct((m, nq), F32), jax.ShapeDtypeStruct((m, nz), F32),
                   jax.ShapeDtypeStruct((m, LANE), F32), jax.ShapeDtypeStruct((LANE, m), F32)),
        grid=(m // tm,),
        in_specs=[pl.BlockSpec((tm, d), row), _full((1, d)), _full(w_pad.shape), _full(w_ba_t.shape)],
        out_specs=(pl.BlockSpec((tm, nq), row), pl.BlockSpec((tm, nz), row),
                   pl.BlockSpec((tm, LANE), row), pl.BlockSpec((LANE, tm), lambda i: (0, i))),
        compiler_params=_params("parallel"),
        name="gdn_proj",
    )(x, g_mix.reshape(1, d), w_pad, w_ba_t)


def _gdn_body(qkv_ref, z_ref, ba_ref, bat_ref, buf_ref, s0_ref, cw_ref, vec_ref, cols_ref,
              o_ref, s_ref, ext_sc, *, c, valid, heads, dk, conv_w):
    ci = pl.program_id(1)
    kdim = heads * dk
    dv = s_ref.shape[3]

    @pl.when(ci == 0)
    def _():
        ext_sc[0:8, :] = buf_ref[0]
        s_ref[...] = s0_ref[...]

    ext_sc[8:8 + c, :] = qkv_ref[0]
    y = None
    for i in range(conv_w):
        lo = 8 - (conv_w - 1) + i
        term = ext_sc[lo:lo + c, :] * cw_ref[i:i + 1, :]
        y = term if y is None else y + term
    ext_sc[0:8, :] = ext_sc[c:c + 8, :]
    y = y * jax.nn.sigmoid(y)

    row = lax.broadcasted_iota(jnp.int32, (c, c), 0)
    col = lax.broadcasted_iota(jnp.int32, (c, c), 1)
    incl = row >= col
    strict = row > col
    eye = (row == col).astype(F32)
    blk_masks = []
    bsz = GDN_INV_BLOCK
    while bsz <= c:
        blk_masks.append((row // bsz) == (col // bsz))
        bsz *= 2
    lower = incl.astype(BF16)
    upper = (row <= col).astype(BF16)

    ba = ba_ref[0]
    a_log = vec_ref[0:1, :]
    dt_bias = vec_ref[1:2, :]
    norm_w = vec_ref[2:3, :]
    beta_all = jax.nn.sigmoid(ba)
    g_col = -jnp.exp(a_log) * _softplus(ba + dt_bias)
    bat = bat_ref[...]
    g_row = -jnp.exp(cols_ref[0]) * _softplus(bat + cols_ref[1])
    if valid < c:
        ok_col = lax.broadcasted_iota(jnp.int32, ba.shape, 0) < valid
        beta_all = jnp.where(ok_col, beta_all, 0.0)
        g_col = jnp.where(ok_col, g_col, 0.0)
        g_row = jnp.where(lax.broadcasted_iota(jnp.int32, bat.shape, 1) < valid, g_row, 0.0)
    gc_col = sum(_dot(lower, p) for p in _split3(g_col))
    gc_row = sum(_dot(p, upper) for p in _split3(g_row))

    hs = range(heads)
    l2n = lambda x: x * lax.rsqrt(jnp.sum(x * x, axis=1, keepdims=True) + NORM_EPS)
    q = [l2n(y[:, h * dk:(h + 1) * dk]) * (dk ** -0.5) for h in hs]
    k = [l2n(y[:, kdim + h * dk:kdim + (h + 1) * dk]) for h in hs]
    v = [y[:, 2 * kdim + h * dv:2 * kdim + (h + 1) * dv] for h in hs]
    beta = [beta_all[:, h:h + 1] for h in hs]
    gcc = [gc_col[:, heads + h:heads + h + 1] for h in hs]
    gcr = [gc_row[heads + h:heads + h + 1, :] for h in hs]
    decay = [jnp.where(incl, jnp.exp(jnp.where(incl, gcc[h] - gcr[h], 0.0)), 0.0) for h in hs]
    kb = [k[h] * beta[h] for h in hs]
    egc = [jnp.exp(gcc[h]) for h in hs]
    gram = [_dot_nt(jnp.concatenate([kb[h], q[h]], axis=0).astype(BF16), k[h].astype(BF16)) for h in hs]
    neg_m = [jnp.where(strict, -(gram[h][:c] * decay[h]), 0.0) for h in hs]
    qk = [jnp.where(incl, gram[h][c:] * decay[h], 0.0).astype(BF16) for h in hs]
    nb = [jnp.where(blk_masks[0], neg_m[h], 0.0) for h in hs]
    t_inv = [eye + nb[h] for h in hs]
    pw = [_dot3(nb[h], nb[h]) for h in hs]
    span = 2
    while span < GDN_INV_BLOCK:
        last = span * 2 >= GDN_INV_BLOCK
        prod = [_dot3(pw[h], t_inv[h] if last else jnp.concatenate([t_inv[h], pw[h]], axis=1)) for h in hs]
        t_inv = [t_inv[h] + prod[h][:, :c] for h in hs]
        if not last:
            pw = [prod[h][:, c:] for h in hs]
        span *= 2
    for lvl in range(1, len(blk_masks)):
        if valid <= GDN_INV_BLOCK << (lvl - 1):
            break
        ring = jnp.logical_and(blk_masks[lvl], jnp.logical_not(blk_masks[lvl - 1]))
        t_b = [t_inv[h].astype(BF16) for h in hs]
        inner = [_dot(jnp.where(ring, neg_m[h], 0.0).astype(BF16), t_b[h]).astype(BF16) for h in hs]
        t_inv = [t_inv[h] + _dot(t_b[h], inner[h]) for h in hs]
    uw = [_dot(t_inv[h].astype(BF16),
               jnp.concatenate([v[h] * beta[h], kb[h] * egc[h]], axis=1).astype(BF16)) for h in hs]
    s_old = [s_ref[0, h] for h in hs]
    ws_qs = [_dot(jnp.concatenate([uw[h][:, dv:], q[h] * egc[h]], axis=0).astype(BF16),
                  s_old[h].astype(BF16)) for h in hs]
    v_new = [(uw[h][:, :dv] - ws_qs[h][:c]).astype(BF16) for h in hs]
    o = [ws_qs[h][c:] + _dot(qk[h], v_new[h]) for h in hs]
    g_last = [gcc[h][c - 1:c, :] for h in hs]
    k_dec_t = [(k[h].T * jnp.exp(g_last[h] - gcr[h])).astype(BF16) for h in hs]
    for h in hs:
        s_ref[0, h] = s_old[h] * jnp.exp(g_last[h]) + _dot(k_dec_t[h], v_new[h])
    for h in hs:
        gate = z_ref[0, :, h * dv:(h + 1) * dv]
        o_ref[0, :, h * dv:(h + 1) * dv] = (
            _rms(o[h], norm_w) * (gate * jax.nn.sigmoid(gate))).astype(o_ref.dtype)


def _gdn_core(qkv, z, ba, bat, buf8, s0, conv_w8, vec, cols, valid, conv_w):
    b, tc, cdim = qkv.shape
    _, heads, dk, dv = s0.shape
    c = GDN_CHUNK
    nc = tc // c
    return pl.pallas_call(
        functools.partial(_gdn_body, c=c, valid=valid, heads=heads, dk=dk, conv_w=conv_w),
        out_shape=(jax.ShapeDtypeStruct((b, tc, heads * dv), BF16),
                   jax.ShapeDtypeStruct(s0.shape, F32)),
        grid=(b, nc),
        in_specs=[pl.BlockSpec((1, c, cdim), lambda bi, ci: (bi, ci, 0)),
                  pl.BlockSpec((1, c, heads * dv), lambda bi, ci: (bi, ci, 0)),
                  pl.BlockSpec((1, c, LANE), lambda bi, ci: (bi, ci, 0)),
                  pl.BlockSpec((LANE, c), lambda bi, ci: (0, bi * nc + ci)),
                  pl.BlockSpec((1, 8, cdim), lambda bi, ci: (bi, 0, 0)),
                  pl.BlockSpec((1, heads, dk, dv), lambda bi, ci: (bi, 0, 0, 0)),
                  _full(conv_w8.shape), _full(vec.shape), _full(cols.shape)],
        out_specs=(pl.BlockSpec((1, c, heads * dv), lambda bi, ci: (bi, ci, 0)),
                   pl.BlockSpec((1, heads, dk, dv), lambda bi, ci: (bi, 0, 0, 0))),
        scratch_shapes=[pltpu.VMEM((c + 8, cdim), F32)],
        compiler_params=_params("parallel", "arbitrary"),
        name="gdn_core",
    )(qkv, z, ba, bat, buf8, s0, conv_w8, vec, cols)


def _gdn_layer(xp, xs, bp, tp, bs, ts, mem_state, conv_state, g_mix,
               w_in, conv_w, a_log, dt_bias, norm_w, w_o):
    _, heads, dk, dv = mem_state.shape
    cdim = conv_state.shape[2]
    val = heads * dv
    kw = conv_w.shape[0]
    c = GDN_CHUNK
    d = w_in.shape[0]
    n_in = w_in.shape[1]
    w_pad = jnp.pad(w_in, ((0, 0), (0, cdim + val + LANE - n_in))).astype(BF16)
    w_ba_t = jnp.transpose(w_pad[:, cdim + val:])
    conv_w8 = jnp.pad(conv_w, ((0, 8 - kw), (0, 0)))
    lane_pad = lambda v, off: jnp.pad(v, (off, LANE - off - v.shape[0]))
    vec = jnp.stack([lane_pad(a_log, heads), lane_pad(dt_bias, heads), norm_w]
                    + [jnp.zeros((LANE,), F32)] * 5)
    cols = jnp.stack([jnp.broadcast_to(lane_pad(a_log, heads)[:, None], (LANE, c)),
                      jnp.broadcast_to(lane_pad(dt_bias, heads)[:, None], (LANE, c))])
    w_o_b = w_o.astype(BF16)

    def run(x, b, t, buf, s0):
        qkv, z, ba, bat = _gdn_proj(x, g_mix, w_pad, w_ba_t, cdim, val)
        qkv3 = qkv.reshape(b, t, cdim)
        xp_rows = jnp.concatenate([buf, qkv3], axis=1)
        new_buf = xp_rows[:, xp_rows.shape[1] - (kw - 1):]
        buf8 = jnp.pad(buf, ((0, 0), (8 - (kw - 1), 0), (0, 0)))
        if t % c == 0:
            valid = c
            z3, ba3 = z.reshape(b, t, val), ba.reshape(b, t, LANE)
        else:
            assert t < c
            valid = t
            padt = lambda a: jnp.pad(a, ((0, 0), (0, c - t), (0, 0)))
            qkv3, z3, ba3 = padt(qkv3), padt(z.reshape(b, t, val)), padt(ba.reshape(b, t, LANE))
            bat = jnp.pad(bat.reshape(LANE, b, t), ((0, 0), (0, 0), (0, c - t))).reshape(LANE, b * c)
        o, s_new = _gdn_core(qkv3, z3, ba3, bat, buf8, s0, conv_w8, vec, cols, valid, kw)
        o = o[:, :t].reshape(b * t, val)
        return _out_proj(x, o, w_o_b), new_buf, s_new

    yp, conv_p, mem_p = run(xp, bp, tp, jnp.zeros((bp, kw - 1, cdim), F32),
                            jnp.zeros((bp, heads, dk, dv), F32))
    ys, conv_s, mem_s = run(xs, bs, ts, conv_state, mem_state)
    return yp, ys, mem_p, conv_p, mem_s, conv_s


def kernel(x_prompt, x_sample, cache_mla_ckv, cache_mla_krope, cache_sb_k, cache_sb_v, state_gdn_mem, state_gdn_conv, page_table, norm_ffn, ffn_w_gate, ffn_w_up, ffn_w_down, norm_mix, norm_final, mla_w_in, mla_norm_q, mla_norm_kv, mla_w_uq, mla_w_uk, mla_w_uv, mla_w_o, sb_w_in, sb_w_o, gdn_w_in, gdn_conv_w, gdn_a_log, gdn_dt_bias, gdn_norm_w, gdn_w_o):
    bp, tp, d = x_prompt.shape
    bs, ts, _ = x_sample.shape
    depth = norm_mix.shape[0]
    xp = x_prompt.reshape(bp * tp, d)
    xs = x_sample.reshape(bs * ts, d)
    wg = ffn_w_gate.astype(BF16)
    wu = ffn_w_up.astype(BF16)
    wd = ffn_w_down.astype(BF16)
    outs = {k: [] for k in ("mla_p", "mla_s", "sb_p", "sb_s", "gdn_p", "gdn_s")}
    for layer in range(depth):
        kind, idx = layer % 3, layer // 3
        ffn0 = (norm_ffn[layer, 0], wg[layer, 0], wu[layer, 0], wd[layer, 0])
        ffn1 = (norm_ffn[layer, 1], wg[layer, 1], wu[layer, 1], wd[layer, 1])
        xp, xs = _ffn_half(xp, *ffn0), _ffn_half(xs, *ffn0)
        if kind == 0:
            xp, xs, a, b_, c_, d_ = _mla_layer(
                xp, xs, bp, tp, bs, ts, cache_mla_ckv, cache_mla_krope, idx, page_table, norm_mix[layer],
                mla_w_in[idx], mla_norm_q[idx], mla_norm_kv[idx], mla_w_uq[idx], mla_w_uk[idx],
                mla_w_uv[idx], mla_w_o[idx])
            outs["mla_p"].append((a, b_))
            outs["mla_s"].append((c_, d_))
        elif kind == 1:
            xp, xs, a, b_, c_, d_ = _sb_layer(
                xp, xs, bp, tp, bs, ts, cache_sb_k, cache_sb_v, idx, page_table, norm_mix[layer],
                sb_w_in[idx], sb_w_o[idx])
            outs["sb_p"].append((a, b_))
            outs["sb_s"].append((c_, d_))
        else:
            xp, xs, a, b_, c_, d_ = _gdn_layer(
                xp, xs, bp, tp, bs, ts, state_gdn_mem[idx], state_gdn_conv[idx], norm_mix[layer],
                gdn_w_in[idx], gdn_conv_w[idx], gdn_a_log[idx], gdn_dt_bias[idx], gdn_norm_w[idx],
                gdn_w_o[idx])
            outs["gdn_p"].append((a, b_))
            outs["gdn_s"].append((c_, d_))
        fin = norm_final if layer == depth - 1 else None
        xp, xs = _ffn_half(xp, *ffn1, final_g=fin), _ffn_half(xs, *ffn1, final_g=fin)

    def stacked(key, i):
        return jnp.stack([o[i] for o in outs[key]])

    return (xp.reshape(bp, tp, d), xs.reshape(bs, ts, d),
            stacked("mla_p", 0), stacked("mla_p", 1), stacked("sb_p", 0), stacked("sb_p", 1),
            stacked("gdn_p", 0), stacked("gdn_p", 1),
            stacked("mla_s", 0), stacked("mla_s", 1), stacked("sb_s", 0), stacked("sb_s", 1),
            stacked("gdn_s", 0), stacked("gdn_s", 1))
```

```python
import functools
import math

import jax
import jax.numpy as jnp
from jax import lax
from jax.experimental import pallas as pl
from jax.experimental.pallas import tpu as pltpu

F32 = jnp.float32
BF16 = jnp.bfloat16

NORM_EPS = 1e-6
ROPE_THETA = 10000.0
LANE = 128
VMEM_LIMIT = 56 * 1024 * 1024

MLA_D_NOPE = 128
MLA_D_ROPE = 64
GDN_CHUNK = 128
GDN_INV_BLOCK = 16
FFN_CHUNK = 256


def _params(*sem):
    return pltpu.CompilerParams(dimension_semantics=sem, vmem_limit_bytes=VMEM_LIMIT)


def _tile(m, pref=512):
    t = pref
    while m % t:
        t //= 2
    return t


def _rms(x, w):
    return x * lax.rsqrt(jnp.mean(x * x, axis=-1, keepdims=True) + NORM_EPS) * w


def _dot(a, b):
    return jnp.dot(a, b, preferred_element_type=F32)


def _dot_nt(a, b):
    return lax.dot_general(a, b, (((1,), (1,)), ((), ())), preferred_element_type=F32)


def _softplus(z):
    return jnp.maximum(z, 0.0) + jnp.log(1.0 + jnp.exp(-jnp.abs(z)))


def _split2(x):
    hi = x.astype(BF16)
    lo = (x - hi.astype(F32)).astype(BF16)
    return hi, lo


def _dot3(a, b):
    a_hi, a_lo = _split2(a)
    b_hi, b_lo = _split2(b)
    return _dot(a_hi, b_hi) + _dot(a_hi, b_lo) + _dot(a_lo, b_hi)


def _split3(x):
    hi = x.astype(BF16)
    r = x - hi.astype(F32)
    mid = r.astype(BF16)
    lo = (r - mid.astype(F32)).astype(BF16)
    return hi, mid, lo


def _full(shape, single=False):
    n = len(shape)
    if single:
        return pl.BlockSpec(shape, lambda *_: (0,) * n, pipeline_mode=pl.Buffered(1))
    return pl.BlockSpec(shape, lambda *_: (0,) * n)


def _ffn_body(x_ref, g_ref, wg_ref, wu_ref, wd_ref, *rest, n_chunks, final):
    if final:
        gf_ref, o_ref = rest
    else:
        (o_ref,) = rest
    x = x_ref[...]
    h = _rms(x, g_ref[...]).astype(BF16)
    acc = jnp.zeros_like(x)
    for c in range(n_chunks):
        sl = slice(c * FFN_CHUNK, (c + 1) * FFN_CHUNK)
        gate = _dot(h, wg_ref[:, sl])
        up = _dot(h, wu_ref[:, sl])
        a = (gate * jax.nn.sigmoid(gate) * up).astype(BF16)
        acc = acc + _dot(a, wd_ref[sl, :])
    y = x + 0.5 * acc
    if final:
        y = _rms(y, gf_ref[...])
    o_ref[...] = y


def _ffn_half(x, g, wg, wu, wd, final_g=None):
    m, d = x.shape
    f = wg.shape[1]
    tm = _tile(m, 1024)
    final = final_g is not None
    ins = [x, g.reshape(1, d), wg, wu, wd]
    specs = [pl.BlockSpec((tm, d), lambda i: (i, 0)), _full((1, d)),
             _full((d, f), True), _full((d, f), True), _full((f, d), True)]
    if final:
        ins.append(final_g.reshape(1, d))
        specs.append(_full((1, d)))
    return pl.pallas_call(
        functools.partial(_ffn_body, n_chunks=f // FFN_CHUNK, final=final),
        out_shape=jax.ShapeDtypeStruct((m, d), F32),
        grid=(m // tm,),
        in_specs=specs,
        out_specs=pl.BlockSpec((tm, d), lambda i: (i, 0)),
        compiler_params=_params("parallel"),
        name="ffn_half",
    )(*ins)


def _oproj_body(x_ref, o_ref, w_ref, y_ref):
    y_ref[...] = x_ref[...] + _dot(o_ref[...], w_ref[...])


def _out_proj(x, o, w):
    m, d = x.shape
    k = o.shape[1]
    tm = _tile(m)
    return pl.pallas_call(
        _oproj_body,
        out_shape=jax.ShapeDtypeStruct((m, d), F32),
        grid=(m // tm,),
        in_specs=[pl.BlockSpec((tm, d), lambda i: (i, 0)),
                  pl.BlockSpec((tm, k), lambda i: (i, 0)), _full((k, d))],
        out_specs=pl.BlockSpec((tm, d), lambda i: (i, 0)),
        compiler_params=_params("parallel"),
        name="out_proj",
    )(x, o, w)


def _rot_cols(w):
    half = w.shape[-1] // 2
    return jnp.concatenate([-w[..., half:], w[..., :half]], axis=-1)


def _rope_tables(pos):
    half = MLA_D_ROPE // 2
    inv = ROPE_THETA ** (-jnp.arange(half, dtype=F32) / half)
    ang = pos.astype(F32)[:, None] * inv[None, :]
    return jnp.tile(jnp.cos(ang), (1, 4)), jnp.tile(jnp.sin(ang), (1, 4))


def _mla_weights(w_in, w_uq, w_uk, w_uv, w_o):
    ql, kvl = w_uq.shape[0], w_uk.shape[0]
    heads = w_uq.shape[1]
    wq, wkv, wkr = w_in[:, :ql], w_in[:, ql:ql + kvl], w_in[:, ql + kvl:]
    wkr_rot = _rot_cols(wkr)
    w_in_ext = jnp.concatenate([wq, wkv, wkr, wkr, wkr_rot, wkr_rot], axis=1).astype(BF16)
    w_nope = w_uq[:, :, :MLA_D_NOPE].reshape(ql, heads * MLA_D_NOPE)
    w_rope = w_uq[:, :, MLA_D_NOPE:]
    w_uq_all = jnp.concatenate(
        [w_nope, w_rope.reshape(ql, heads * MLA_D_ROPE),
         _rot_cols(w_rope).reshape(ql, heads * MLA_D_ROPE)], axis=1).astype(BF16)
    w_uk_t = jnp.transpose(w_uk, (1, 2, 0)).astype(BF16)
    w_uv_h = jnp.transpose(w_uv, (1, 0, 2)).astype(BF16)
    return w_in_ext, w_uq_all, w_uk_t, w_uv_h, w_o.astype(BF16)


def _mla_proj_body(x_ref, gm_ref, win_ref, nq_ref, nkv_ref, wuq_ref, wuk_ref, cos_ref, sin_ref,
                   q_ref, ckv_ref, kr_ref, kcat_ref, *, heads, scale):
    ql = nq_ref.shape[1]
    kvl = nkv_ref.shape[1]
    h = _rms(x_ref[...], gm_ref[...]).astype(BF16)
    proj = _dot(h, win_ref[...])
    cq = _rms(proj[:, :ql], nq_ref[...]).astype(BF16)
    ckv = _rms(proj[:, ql:ql + kvl], nkv_ref[...])
    cos = cos_ref[...]
    sin = sin_ref[...]
    o = ql + kvl
    kr2 = proj[:, o:o + LANE] * cos + proj[:, o + LANE:o + 2 * LANE] * sin
    ckv_ref[...] = ckv
    kr_ref[...] = kr2[:, :MLA_D_ROPE]
    kcat_ref[:, :kvl] = ckv.astype(BF16)
    kcat_ref[:, kvl:] = kr2.astype(BF16)
    lane = lax.broadcasted_iota(jnp.int32, cos.shape, 1)
    first = lane < MLA_D_ROPE
    nope_cols = heads * MLA_D_NOPE
    rope_cols = heads * MLA_D_ROPE
    for j in range(heads // 2):
        c0 = nope_cols + j * LANE
        qr = _dot(cq, wuq_ref[:, c0:c0 + LANE])
        qrot = _dot(cq, wuq_ref[:, c0 + rope_cols:c0 + rope_cols + LANE])
        qr2 = (qr * cos + qrot * sin) * scale
        for hh in range(2):
            hd = 2 * j + hh
            qn = _dot(cq, wuq_ref[:, hd * MLA_D_NOPE:(hd + 1) * MLA_D_NOPE]).astype(BF16)
            qlat = _dot(qn, wuk_ref[hd]) * scale
            q_ref[hd, :, :kvl] = qlat.astype(BF16)
            keep = first if hh == 0 else jnp.logical_not(first)
            q_ref[hd, :, kvl:] = jnp.where(keep, qr2, 0.0).astype(BF16)


def _mla_proj(x, g_mix, w_in_ext, norm_q, norm_kv, w_uq_all, w_uk_t, cos, sin):
    m, d = x.shape
    heads, _, kvl = w_uk_t.shape
    ql = norm_q.shape[0]
    tm = _tile(m)
    scale = (MLA_D_NOPE + MLA_D_ROPE) ** -0.5
    qw = kvl + LANE
    row = lambda i: (i, 0)
    return pl.pallas_call(
        functools.partial(_mla_proj_body, heads=heads, scale=scale),
        out_shape=(jax.ShapeDtypeStruct((heads, m, qw), BF16),
                   jax.ShapeDtypeStruct((m, kvl), F32),
                   jax.ShapeDtypeStruct((m, MLA_D_ROPE), F32),
                   jax.ShapeDtypeStruct((m, qw), BF16)),
        grid=(m // tm,),
        in_specs=[pl.BlockSpec((tm, d), row), _full((1, d)), _full(w_in_ext.shape),
                  _full((1, ql)), _full((1, kvl)), _full(w_uq_all.shape), _full(w_uk_t.shape),
                  pl.BlockSpec((tm, LANE), row), pl.BlockSpec((tm, LANE), row)],
        out_specs=(pl.BlockSpec((heads, tm, qw), lambda i: (0, i, 0)),
                   pl.BlockSpec((tm, kvl), row), pl.BlockSpec((tm, MLA_D_ROPE), row),
                   pl.BlockSpec((tm, qw), row)),
        compiler_params=_params("parallel"),
        name="mla_proj",
    )(x, g_mix.reshape(1, d), w_in_ext, norm_q.reshape(1, ql), norm_kv.reshape(1, kvl),
      w_uq_all, w_uk_t, cos, sin)


def _softmax_update(s, v, m_ref, l_ref, acc_ref, rows):
    nb = s.shape[1] // LANE
    m_prev = m_ref[rows]
    m_new = jnp.maximum(m_prev, jnp.max(s, axis=1, keepdims=True))
    alpha = jnp.exp(m_prev - m_new)
    ps = [jnp.exp(s[:, c * LANE:(c + 1) * LANE] - m_new) for c in range(nb)]
    psum = ps[0]
    for c in range(1, nb):
        psum = psum + ps[c]
    l_ref[rows] = alpha * l_ref[rows] + jnp.sum(psum, axis=1, keepdims=True)
    p = ps[0].astype(BF16) if nb == 1 else jnp.concatenate([x.astype(BF16) for x in ps], axis=1)
    reps = acc_ref.shape[1] // LANE
    acc_ref[rows] = jnp.concatenate([alpha] * reps, axis=1) * acc_ref[rows] + _dot(p, v)
    m_ref[rows] = m_new


def _softmax_result(l_ref, acc_ref):
    inv = 1.0 / l_ref[...]
    return acc_ref[...] * jnp.concatenate([inv] * (acc_ref.shape[1] // LANE), axis=1)


def _softmax_init(m_ref, l_ref, acc_ref):
    m_ref[...] = jnp.full(m_ref.shape, -jnp.inf, F32)
    l_ref[...] = jnp.zeros(l_ref.shape, F32)
    acc_ref[...] = jnp.zeros(acc_ref.shape, F32)


def _mla_prompt_body(q_ref, k_ref, o_ref, m_sc, l_sc, acc_sc, *, tq, tk, kvl):
    i = pl.program_id(1)
    heads = q_ref.shape[0]
    _softmax_init(m_sc, l_sc, acc_sc)
    n_slabs = 2
    hs = heads // n_slabs

    def step(j, width, masked):
        k = k_ref[0, pl.ds(pl.multiple_of(j * tk, tk), width), :]
        for sl in range(n_slabs):
            q = q_ref[sl * hs:(sl + 1) * hs].reshape(hs * tq, q_ref.shape[-1])
            s = _dot_nt(q, k)
            if masked:
                qpos = i * tq + (lax.broadcasted_iota(jnp.int32, s.shape, 0) & (tq - 1))
                kpos = j * tk + lax.broadcasted_iota(jnp.int32, s.shape, 1)
                s = jnp.where(kpos <= qpos, s, -1e30)
            _softmax_update(s, k[:, :kvl], m_sc, l_sc, acc_sc, pl.ds(sl * hs * tq, hs * tq))

    per = tk // tq
    n_full = i // per

    def loop_body(j, carry):
        step(j, tk, False)
        return carry

    lax.fori_loop(0, n_full, loop_body, 0)
    for r in range(per):
        @pl.when(i % per == r)
        def _(r=r):
            step(n_full, (r + 1) * tq, True)

    o_ref[...] = _softmax_result(l_sc, acc_sc).astype(o_ref.dtype).reshape(o_ref.shape)


def _mla_prompt_attn(q, kcat, b, t, kvl):
    heads, m, qw = q.shape
    tq = 128
    tk = min(512, t)
    nq = t // tq
    q4 = q.reshape(heads, b, t, qw)
    k3 = kcat.reshape(b, t, qw)
    out = pl.pallas_call(
        functools.partial(_mla_prompt_body, tq=tq, tk=tk, kvl=kvl),
        out_shape=jax.ShapeDtypeStruct((heads, b, t, kvl), BF16),
        grid=(b, nq),
        in_specs=[pl.BlockSpec((heads, 1, tq, qw), lambda bi, i: (0, bi, i, 0)),
                  pl.BlockSpec((1, t, qw), lambda bi, i: (bi, 0, 0))],
        out_specs=pl.BlockSpec((heads, 1, tq, kvl), lambda bi, i: (0, bi, i, 0)),
        scratch_shapes=[pltpu.VMEM((heads * tq, LANE), F32), pltpu.VMEM((heads * tq, LANE), F32),
                        pltpu.VMEM((heads * tq, kvl), F32)],
        compiler_params=_params("parallel", "arbitrary"),
        name="mla_prompt_attn",
    )(q4, k3)
    return out.reshape(heads, m, kvl)


def _mla_decode_body(pt_ref, ql_ref, qr_ref, *refs, n_pg, t_new):
    del pt_ref
    ckv_refs = refs[:n_pg]
    krt_refs = refs[n_pg:2 * n_pg]
    nl_ref, nrt_ref, o_ref, m_sc, l_sc, acc_sc = refs[2 * n_pg:]
    j = pl.program_id(1)

    @pl.when(j == 0)
    def _():
        _softmax_init(m_sc, l_sc, acc_sc)

    ql = ql_ref[0]
    qr = qr_ref[0]
    rows = pl.ds(0, m_sc.shape[0])
    kv = jnp.concatenate([r[...].astype(BF16) for r in ckv_refs], axis=0)
    krt = jnp.concatenate([r[...].astype(BF16) for r in krt_refs], axis=1)
    _softmax_update(_dot_nt(ql, kv) + _dot(qr, krt), kv, m_sc, l_sc, acc_sc, rows)

    @pl.when(j == pl.num_programs(1) - 1)
    def _():
        kn = nl_ref[0]
        s = _dot_nt(ql, kn) + _dot(qr, nrt_ref[0])
        tq = lax.broadcasted_iota(jnp.int32, s.shape, 0) % t_new
        col = lax.broadcasted_iota(jnp.int32, s.shape, 1)
        s = jnp.where(col <= tq, s, -1e30)
        _softmax_update(s, kn, m_sc, l_sc, acc_sc, rows)
        o_ref[0] = _softmax_result(l_sc, acc_sc).astype(o_ref.dtype)


def _mla_decode_attn(q_lat, q_rope, cache_ckv, cache_krt, layer, page_table, new_lat, new_rope_t, t_new):
    b, rows, kvl = q_lat.shape
    n_pages = page_table.shape[1]
    page = cache_ckv.shape[2]
    n_pg = 32
    while n_pages % n_pg:
        n_pg //= 2

    def page_spec(shape, n):
        return pl.BlockSpec((None, None) + shape,
                            lambda bi, j, pt: (layer, pt[bi, j * n_pg + n], 0, 0))

    per_b = lambda bi, j, pt: (bi, 0, 0)
    grid_spec = pltpu.PrefetchScalarGridSpec(
        num_scalar_prefetch=1,
        grid=(b, n_pages // n_pg),
        in_specs=([pl.BlockSpec((1, rows, kvl), per_b), pl.BlockSpec((1, rows, MLA_D_ROPE), per_b)]
                  + [page_spec((page, kvl), n) for n in range(n_pg)]
                  + [page_spec((MLA_D_ROPE, page), n) for n in range(n_pg)]
                  + [pl.BlockSpec((1, page, kvl), per_b), pl.BlockSpec((1, MLA_D_ROPE, page), per_b)]),
        out_specs=pl.BlockSpec((1, rows, kvl), per_b),
        scratch_shapes=[pltpu.VMEM((rows, LANE), F32), pltpu.VMEM((rows, LANE), F32),
                        pltpu.VMEM((rows, kvl), F32)],
    )
    return pl.pallas_call(
        functools.partial(_mla_decode_body, n_pg=n_pg, t_new=t_new),
        out_shape=jax.ShapeDtypeStruct((b, rows, kvl), BF16),
        grid_spec=grid_spec,
        compiler_params=_params("parallel", "arbitrary"),
        name="mla_decode_attn",
    )(page_table, q_lat, q_rope, *([cache_ckv] * n_pg), *([cache_krt] * n_pg), new_lat, new_rope_t)


def _mla_out_body(x_ref, o_ref, wuv_ref, wo_ref, y_ref, *, heads):
    dv = wuv_ref.shape[2]
    acc = x_ref[...]
    for j in range(heads // 2):
        a = _dot(o_ref[2 * j], wuv_ref[2 * j]).astype(BF16)
        bb = _dot(o_ref[2 * j + 1], wuv_ref[2 * j + 1]).astype(BF16)
        pair = jnp.concatenate([a, bb], axis=1)
        acc = acc + _dot(pair, wo_ref[2 * j * dv:(2 * j + 2) * dv, :])
    y_ref[...] = acc


def _mla_out(x, o_lat, w_uv_h, w_o):
    m, d = x.shape
    heads, _, kvl = o_lat.shape
    tm = _tile(m)
    return pl.pallas_call(
        functools.partial(_mla_out_body, heads=heads),
        out_shape=jax.ShapeDtypeStruct((m, d), F32),
        grid=(m // tm,),
        in_specs=[pl.BlockSpec((tm, d), lambda i: (i, 0)),
                  pl.BlockSpec((heads, tm, kvl), lambda i: (0, i, 0)),
                  _full(w_uv_h.shape), _full(w_o.shape)],
        out_specs=pl.BlockSpec((tm, d), lambda i: (i, 0)),
        compiler_params=_params("parallel"),
        name="mla_out",
    )(x, o_lat, w_uv_h, w_o)


def _mla_layer(xp, xs, bp, tp, bs, ts, cache_ckv, cache_kr, layer, page_table, g_mix,
               w_in, norm_q, norm_kv, w_uq, w_uk, w_uv, w_o):
    w_in_ext, w_uq_all, w_uk_t, w_uv_h, w_o_b = _mla_weights(w_in, w_uq, w_uk, w_uv, w_o)
    heads, _, kvl = w_uk_t.shape
    page = cache_ckv.shape[2]
    past = page_table.shape[1] * page
    cos_p, sin_p = _rope_tables(jnp.tile(jnp.arange(tp, dtype=jnp.int32), bp))
    q_p, ckv_p, kr_p, kcat_p = _mla_proj(xp, g_mix, w_in_ext, norm_q, norm_kv, w_uq_all, w_uk_t, cos_p, sin_p)
    ol_p = _mla_prompt_attn(q_p, kcat_p, bp, tp, kvl)
    yp = _mla_out(xp, ol_p, w_uv_h, w_o_b)
    cos_s, sin_s = _rope_tables(jnp.tile(past + jnp.arange(ts, dtype=jnp.int32), bs))
    q_s, ckv_s, kr_s, kcat_s = _mla_proj(xs, g_mix, w_in_ext, norm_q, norm_kv, w_uq_all, w_uk_t, cos_s, sin_s)
    q_s = jnp.transpose(q_s.reshape(heads, bs, ts, kvl + LANE), (1, 0, 2, 3)).reshape(bs, heads * ts, kvl + LANE)
    q_lat = q_s[..., :kvl]
    q_rope = q_s[..., kvl:kvl + MLA_D_ROPE] + q_s[..., kvl + MLA_D_ROPE:]
    kcat_s = kcat_s.reshape(bs, ts, kvl + LANE)
    new_lat = jnp.pad(kcat_s[..., :kvl], ((0, 0), (0, page - ts), (0, 0)))
    new_rope_t = jnp.pad(jnp.swapaxes(kcat_s[..., kvl:kvl + MLA_D_ROPE], 1, 2),
                         ((0, 0), (0, 0), (0, page - ts)))
    cache_krt = jnp.swapaxes(cache_kr, 2, 3)
    ol_s = _mla_decode_attn(q_lat, q_rope, cache_ckv, cache_krt, layer, page_table, new_lat, new_rope_t, ts)
    ol_s = jnp.transpose(ol_s.reshape(bs, heads, ts, kvl), (1, 0, 2, 3)).reshape(heads, bs * ts, kvl)
    ys = _mla_out(xs, ol_s, w_uv_h, w_o_b)
    return (yp, ys, ckv_p.reshape(bp, tp, kvl), kr_p.reshape(bp, tp, MLA_D_ROPE),
            ckv_s.reshape(bs, ts, kvl), kr_s.reshape(bs, ts, MLA_D_ROPE))


def _sb_proj_body(x_ref, g_ref, w_ref, q_ref, k_ref, v_ref, kb_ref, vb_ref, *, scale):
    n = q_ref.shape[1]
    h = _rms(x_ref[...], g_ref[...]).astype(BF16)
    q_ref[...] = (_dot(h, w_ref[:, :n]) * scale).astype(BF16)
    k = _dot(h, w_ref[:, n:2 * n])
    k_ref[...] = k
    kb_ref[...] = k.astype(BF16)
    v = _dot(h, w_ref[:, 2 * n:])
    v_ref[...] = v
    vb_ref[...] = v.astype(BF16)


def _sb_proj(x, g_mix, w_in, scale):
    m, d = x.shape
    n = w_in.shape[1] // 3
    tm = _tile(m)
    row = lambda i: (i, 0)
    return pl.pallas_call(
        functools.partial(_sb_proj_body, scale=scale),
        out_shape=(jax.ShapeDtypeStruct((m, n), BF16), jax.ShapeDtypeStruct((m, n), F32),
                   jax.ShapeDtypeStruct((m, n), F32), jax.ShapeDtypeStruct((m, n), BF16),
                   jax.ShapeDtypeStruct((m, n), BF16)),
        grid=(m // tm,),
        in_specs=[pl.BlockSpec((tm, d), row), _full((1, d)), _full(w_in.shape)],
        out_specs=tuple(pl.BlockSpec((tm, n), row) for _ in range(5)),
        compiler_params=_params("parallel"),
        name="sb_proj",
    )(x, g_mix.reshape(1, d), w_in)


def _sb_terms(z):
    l = jnp.log(1.0 + jnp.exp(-jnp.abs(z)))
    return jnp.maximum(z, 0.0) + l, jnp.minimum(z, 0.0) - l


def _strict_later(n):
    r = lax.broadcasted_iota(jnp.int32, (n, n), 0)
    c = lax.broadcasted_iota(jnp.int32, (n, n), 1)
    return (r > c).astype(BF16)


def _later_and_total(n):
    r = lax.broadcasted_iota(jnp.int32, (n, n + LANE), 0)
    c = lax.broadcasted_iota(jnp.int32, (n, n + LANE), 1)
    return jnp.logical_or(r > c, c >= n).astype(BF16)


def _sb_prompt_body(q_ref, k_ref, v_ref, w_ref, o_ref, acc_sc, car_sc, *, tq, hd, n_pairs):
    i = pl.program_id(2)
    acc_sc[...] = jnp.zeros(acc_sc.shape, F32)
    car_sc[...] = jnp.zeros(car_sc.shape, F32)
    first = lax.broadcasted_iota(jnp.int32, (tq, LANE), 1) < hd
    heads = range(2 * n_pairs)

    def block(j, masked):
        start = pl.multiple_of(j * tq, tq)
        w = w_ref[...]
        if masked:
            causal = (lax.broadcasted_iota(jnp.int32, (tq, tq), 1)
                      < lax.broadcasted_iota(jnp.int32, (tq, tq), 0))
        zs, vs = [], []
        for p in range(n_pairs):
            lanes = slice(p * LANE, (p + 1) * LANE)
            q2 = q_ref[0, :, lanes]
            k = k_ref[0, pl.ds(start, tq), lanes]
            zero = jnp.zeros_like(q2)
            for n in range(2):
                qh = jnp.where(first if n == 0 else jnp.logical_not(first), q2, zero)
                zs.append(_dot_nt(qh, k))
                vs.append(v_ref[0, pl.ds(start, tq), lanes])
        sps, lbs = [], []
        for z in zs:
            sp, log_beta = _sb_terms(z)
            if masked:
                sp = jnp.where(causal, sp, 0.0)
            sps.append(sp)
            lbs.append(log_beta)
        css = []
        for sp in sps:
            hi, lo = _split2(sp)
            css.append(_dot(jnp.concatenate([hi, lo], axis=1), w))
        for idx in heads:
            car = car_sc[idx]
            parts = []
            for c in range(tq // LANE):
                cl = slice(c * LANE, (c + 1) * LANE)
                parts.append(jnp.exp(lbs[idx][:, cl] - css[idx][:, cl] - car))
            a = jnp.concatenate(parts, axis=1)
            if masked:
                a = jnp.where(causal, a, 0.0)
            acc_sc[idx] = acc_sc[idx] + _dot(a.astype(BF16), vs[idx])
            car_sc[idx] = car + jnp.sum(sps[idx], axis=1, keepdims=True)

    block(i, True)

    def loop_body(jj, carry):
        block(i - 1 - jj, False)
        return carry

    lax.fori_loop(0, i, loop_body, 0)
    for p in range(n_pairs):
        o_ref[0, :, p * LANE:(p + 1) * LANE] = jnp.where(
            first, acc_sc[2 * p], acc_sc[2 * p + 1]).astype(o_ref.dtype)


def _sb_prompt_attn(q, kb, vb, b, t, hd):
    m, n = q.shape
    tq = min(256, t)
    nq = t // tq
    n_pairs = 4
    gw = n_pairs * LANE
    out = pl.pallas_call(
        functools.partial(_sb_prompt_body, tq=tq, hd=hd, n_pairs=n_pairs),
        out_shape=jax.ShapeDtypeStruct((b, t, n), BF16),
        grid=(b, n // gw, nq),
        in_specs=[pl.BlockSpec((1, tq, gw), lambda bi, p, i: (bi, i, p)),
                  pl.BlockSpec((1, t, gw), lambda bi, p, i: (bi, 0, p)),
                  pl.BlockSpec((1, t, gw), lambda bi, p, i: (bi, 0, p)),
                  _full((2 * tq, tq))],
        out_specs=pl.BlockSpec((1, tq, gw), lambda bi, p, i: (bi, i, p)),
        scratch_shapes=[pltpu.VMEM((2 * n_pairs, tq, LANE), F32), pltpu.VMEM((2 * n_pairs, tq, LANE), F32)],
        compiler_params=_params("parallel", "parallel", "arbitrary"),
        name="sb_prompt_attn",
    )(q.reshape(b, t, n), kb.reshape(b, t, n), vb.reshape(b, t, n), jnp.tile(_strict_later(tq), (2, 1)))
    return out.reshape(m, n)


def _sb_decode_body(pt_ref, q_ref, *refs, n_pg, t_new):
    del pt_ref
    k_refs = refs[:n_pg]
    v_refs = refs[n_pg:2 * n_pg]
    kn_ref, vn_ref, w_ref, o_ref, acc_sc, car_sc = refs[2 * n_pg:]
    jj = pl.program_id(1)
    q = q_ref[0]
    w = w_ref[...]

    @pl.when(jj == 0)
    def _():
        sp, log_beta = _sb_terms(_dot(q, kn_ref[0]))
        causal = (lax.broadcasted_iota(jnp.int32, sp.shape, 1)
                  < lax.broadcasted_iota(jnp.int32, sp.shape, 0) % t_new)
        sp = jnp.where(causal, sp, 0.0)
        hi, lo = _split2(sp)
        cs = _dot(hi, w) + _dot(lo, w)
        a = jnp.where(causal, jnp.exp(log_beta - cs[:, :LANE]), 0.0)
        acc_sc[...] = _dot_nt(a.astype(BF16), vn_ref[0])
        car_sc[...] = cs[:, LANE:]

    kt = jnp.concatenate([r[...].astype(BF16) for r in k_refs], axis=1)
    vt = jnp.concatenate([r[...].astype(BF16) for r in v_refs], axis=1)
    sp, log_beta = _sb_terms(_dot(q, kt))
    hi, lo = _split2(sp)
    css = []
    for n in range(n_pg):
        cl = slice(n * LANE, (n + 1) * LANE)
        css.append(_dot(hi[:, cl], w) + _dot(lo[:, cl], w))
    carry = car_sc[...]
    parts = [None] * n_pg
    for n in reversed(range(n_pg)):
        cl = slice(n * LANE, (n + 1) * LANE)
        parts[n] = jnp.exp(log_beta[:, cl] - css[n][:, :LANE] - carry).astype(BF16)
        carry = carry + css[n][:, LANE:]
    car_sc[...] = carry
    acc_sc[...] = acc_sc[...] + _dot_nt(jnp.concatenate(parts, axis=1), vt)

    @pl.when(jj == pl.num_programs(1) - 1)
    def _():
        o_ref[0] = acc_sc[...]


def _sb_decode_attn(qbd, cache_kt, cache_vt, layer, page_table, k_new_t, v_new_t, t_new):
    b, cols, n = qbd.shape
    n_pages = page_table.shape[1]
    page = cache_kt.shape[3]
    n_pg = 16
    while n_pages % n_pg:
        n_pg //= 2

    def page_spec(i):
        return pl.BlockSpec((None, None, n, page),
                            lambda bi, j, pt: (layer, pt[bi, n_pages - (j + 1) * n_pg + i], 0, 0))

    per_b = lambda bi, j, pt: (bi, 0, 0)
    grid_spec = pltpu.PrefetchScalarGridSpec(
        num_scalar_prefetch=1,
        grid=(b, n_pages // n_pg),
        in_specs=([pl.BlockSpec((1, cols, n), per_b)]
                  + [page_spec(i) for i in range(n_pg)] + [page_spec(i) for i in range(n_pg)]
                  + [pl.BlockSpec((1, n, page), per_b), pl.BlockSpec((1, n, page), per_b),
                     pl.BlockSpec((page, page + LANE), lambda bi, j, pt: (0, 0))]),
        out_specs=pl.BlockSpec((1, cols, n), per_b),
        scratch_shapes=[pltpu.VMEM((cols, n), F32), pltpu.VMEM((cols, LANE), F32)],
    )
    return pl.pallas_call(
        functools.partial(_sb_decode_body, n_pg=n_pg, t_new=t_new),
        out_shape=jax.ShapeDtypeStruct((b, cols, n), F32),
        grid_spec=grid_spec,
        compiler_params=_params("parallel", "arbitrary"),
        name="sb_decode_attn",
    )(page_table, qbd, *([cache_kt] * n_pg), *([cache_vt] * n_pg), k_new_t, v_new_t,
      _later_and_total(page))


def _sb_layer(xp, xs, bp, tp, bs, ts, cache_k, cache_v, layer, page_table, g_mix, w_in, w_o):
    heads, hd = cache_k.shape[3], cache_k.shape[4]
    n = heads * hd
    page = cache_k.shape[2]
    scale = hd ** -0.5
    w_in_b = w_in.astype(BF16)
    w_o_b = w_o.astype(BF16)
    q_p, k_p, v_p, kb_p, vb_p = _sb_proj(xp, g_mix, w_in_b, scale)
    o_p = _sb_prompt_attn(q_p, kb_p, vb_p, bp, tp, hd)
    yp = _out_proj(xp, o_p, w_o_b)
    q_s, k_s, v_s, kb_s, vb_s = _sb_proj(xs, g_mix, w_in_b, scale)
    q4 = q_s.reshape(bs, ts, heads, hd)
    eye = jnp.eye(heads, dtype=BF16)
    qbd = q4[:, :, :, None, :] * eye[None, None, :, :, None]
    qbd = jnp.transpose(qbd, (0, 2, 1, 3, 4)).reshape(bs, heads * ts, n)
    to_t = lambda a: jnp.pad(jnp.swapaxes(a.reshape(bs, ts, n), 1, 2), ((0, 0), (0, 0), (0, page - ts)))
    to_view = lambda c: jnp.transpose(c, (0, 1, 3, 4, 2)).reshape(c.shape[0], c.shape[1], n, page)
    res = _sb_decode_attn(qbd, to_view(cache_k), to_view(cache_v), layer, page_table,
                          to_t(kb_s), to_t(vb_s), ts)
    res = res.reshape(bs, heads, ts, heads, hd)
    idx = jnp.arange(heads)
    o_s = res[:, idx, :, idx, :]
    o_s = jnp.transpose(o_s, (1, 2, 0, 3)).reshape(bs * ts, n).astype(BF16)
    ys = _out_proj(xs, o_s, w_o_b)
    return (yp, ys, k_p.reshape(bp, tp, heads, hd), v_p.reshape(bp, tp, heads, hd),
            k_s.reshape(bs, ts, heads, hd), v_s.reshape(bs, ts, heads, hd))


def _gdn_proj_body(x_ref, g_ref, w_ref, wbat_ref, qkv_ref, z_ref, ba_ref, bat_ref):
    nq = qkv_ref.shape[1]
    nz = z_ref.shape[1]
    h = _rms(x_ref[...], g_ref[...]).astype(BF16)
    for c0 in range(0, nq, 1024):
        qkv_ref[:, c0:c0 + 1024] = _dot(h, w_ref[:, c0:c0 + 1024])
    z_ref[...] = _dot(h, w_ref[:, nq:nq + nz])
    ba_ref[...] = _dot(h, w_ref[:, nq + nz:])
    bat_ref[...] = _dot_nt(wbat_ref[...], h)


def _gdn_proj(x, g_mix, w_pad, w_ba_t, nq, nz):
    m, d = x.shape
    tm = _tile(m)
    row = lambda i: (i, 0)
    return pl.pallas_call(
        _gdn_proj_body,
        out_shape=(jax.ShapeDtypeStruct((m, nq), F32), jax.ShapeDtypeStruct((m, nz), F32),
                   jax.ShapeDtypeStruct((m, LANE), F32), jax.ShapeDtypeStruct((LANE, m), F32)),
        grid=(m // tm,),
        in_specs=[pl.BlockSpec((tm, d), row), _full((1, d)), _full(w_pad.shape), _full(w_ba_t.shape)],
        out_specs=(pl.BlockSpec((tm, nq), row), pl.BlockSpec((tm, nz), row),
                   pl.BlockSpec((tm, LANE), row), pl.BlockSpec((LANE, tm), lambda i: (0, i))),
        compiler_params=_params("parallel"),
        name="gdn_proj",
    )(x, g_mix.reshape(1, d), w_pad, w_ba_t)


def _gdn_body(qkv_ref, z_ref, ba_ref, bat_ref, buf_ref, s0_ref, cw_ref, vec_ref, cols_ref,
              o_ref, s_ref, ext_sc, *, c, valid, heads, dk, conv_w):
    ci = pl.program_id(1)
    kdim = heads * dk
    dv = s_ref.shape[3]

    @pl.when(ci == 0)
    def _():
        ext_sc[0:8, :] = buf_ref[0]
        s_ref[...] = s0_ref[...]

    ext_sc[8:8 + c, :] = qkv_ref[0]
    y = None
    for i in range(conv_w):
        lo = 8 - (conv_w - 1) + i
        term = ext_sc[lo:lo + c, :] * cw_ref[i:i + 1, :]
        y = term if y is None else y + term
    ext_sc[0:8, :] = ext_sc[c:c + 8, :]
    y = y * jax.nn.sigmoid(y)

    row = lax.broadcasted_iota(jnp.int32, (c, c), 0)
    col = lax.broadcasted_iota(jnp.int32, (c, c), 1)
    incl = row >= col
    strict = row > col
    eye = (row == col).astype(F32)
    blk_masks = []
    bsz = GDN_INV_BLOCK
    while bsz <= c:
        blk_masks.append((row // bsz) == (col // bsz))
        bsz *= 2
    lower = incl.astype(BF16)
    upper = (row <= col).astype(BF16)

    ba = ba_ref[0]
    a_log = vec_ref[0:1, :]
    dt_bias = vec_ref[1:2, :]
    norm_w = vec_ref[2:3, :]
    beta_all = jax.nn.sigmoid(ba)
    g_col = -jnp.exp(a_log) * _softplus(ba + dt_bias)
    bat = bat_ref[...]
    g_row = -jnp.exp(cols_ref[0]) * _softplus(bat + cols_ref[1])
    if valid < c:
        ok_col = lax.broadcasted_iota(jnp.int32, ba.shape, 0) < valid
        beta_all = jnp.where(ok_col, beta_all, 0.0)
        g_col = jnp.where(ok_col, g_col, 0.0)
        g_row = jnp.where(lax.broadcasted_iota(jnp.int32, bat.shape, 1) < valid, g_row, 0.0)
    gc_col = sum(_dot(lower, p) for p in _split3(g_col))
    gc_row = sum(_dot(p, upper) for p in _split3(g_row))

    hs = range(heads)
    l2n = lambda x: x * lax.rsqrt(jnp.sum(x * x, axis=1, keepdims=True) + NORM_EPS)
    q = [l2n(y[:, h * dk:(h + 1) * dk]) * (dk ** -0.5) for h in hs]
    k = [l2n(y[:, kdim + h * dk:kdim + (h + 1) * dk]) for h in hs]
    v = [y[:, 2 * kdim + h * dv:2 * kdim + (h + 1) * dv] for h in hs]
    beta = [beta_all[:, h:h + 1] for h in hs]
    gcc = [gc_col[:, heads + h:heads + h + 1] for h in hs]
    gcr = [gc_row[heads + h:heads + h + 1, :] for h in hs]
    decay = [jnp.where(incl, jnp.exp(jnp.where(incl, gcc[h] - gcr[h], 0.0)), 0.0) for h in hs]
    kb = [k[h] * beta[h] for h in hs]
    egc = [jnp.exp(gcc[h]) for h in hs]
    gram = [_dot_nt(jnp.concatenate([kb[h], q[h]], axis=0).astype(BF16), k[h].astype(BF16)) for h in hs]
    neg_m = [jnp.where(strict, -(gram[h][:c] * decay[h]), 0.0) for h in hs]
    qk = [jnp.where(incl, gram[h][c:] * decay[h], 0.0).astype(BF16) for h in hs]
    nb = [jnp.where(blk_masks[0], neg_m[h], 0.0) for h in hs]
    t_inv = [eye + nb[h] for h in hs]
    pw = [_dot3(nb[h], nb[h]) for h in hs]
    span = 2
    while span < GDN_INV_BLOCK:
        last = span * 2 >= GDN_INV_BLOCK
        prod = [_dot3(pw[h], t_inv[h] if last else jnp.concatenate([t_inv[h], pw[h]], axis=1)) for h in hs]
        t_inv = [t_inv[h] + prod[h][:, :c] for h in hs]
        if not last:
            pw = [prod[h][:, c:] for h in hs]
        span *= 2
    for lvl in range(1, len(blk_masks)):
        if valid <= GDN_INV_BLOCK << (lvl - 1):
            break
        ring = jnp.logical_and(blk_masks[lvl], jnp.logical_not(blk_masks[lvl - 1]))
        t_b = [t_inv[h].astype(BF16) for h in hs]
        inner = [_dot(jnp.where(ring, neg_m[h], 0.0).astype(BF16), t_b[h]).astype(BF16) for h in hs]
        t_inv = [t_inv[h] + _dot(t_b[h], inner[h]) for h in hs]
    uw = [_dot(t_inv[h].astype(BF16),
               jnp.concatenate([v[h] * beta[h], kb[h] * egc[h]], axis=1).astype(BF16)) for h in hs]
    s_old = [s_ref[0, h] for h in hs]
    ws_qs = [_dot(jnp.concatenate([uw[h][:, dv:], q[h] * egc[h]], axis=0).astype(BF16),
                  s_old[h].astype(BF16)) for h in hs]
    v_new = [(uw[h][:, :dv] - ws_qs[h][:c]).astype(BF16) for h in hs]
    o = [ws_qs[h][c:] + _dot(qk[h], v_new[h]) for h in hs]
    g_last = [gcc[h][c - 1:c, :] for h in hs]
    k_dec_t = [(k[h].T * jnp.exp(g_last[h] - gcr[h])).astype(BF16) for h in hs]
    for h in hs:
        s_ref[0, h] = s_old[h] * jnp.exp(g_last[h]) + _dot(k_dec_t[h], v_new[h])
    for h in hs:
        gate = z_ref[0, :, h * dv:(h + 1) * dv]
        o_ref[0, :, h * dv:(h + 1) * dv] = (
            _rms(o[h], norm_w) * (gate * jax.nn.sigmoid(gate))).astype(o_ref.dtype)


def _gdn_core(qkv, z, ba, bat, buf8, s0, conv_w8, vec, cols, valid, conv_w):
    b, tc, cdim = qkv.shape
    _, heads, dk, dv = s0.shape
    c = GDN_CHUNK
    nc = tc // c
    return pl.pallas_call(
        functools.partial(_gdn_body, c=c, valid=valid, heads=heads, dk=dk, conv_w=conv_w),
        out_shape=(jax.ShapeDtypeStruct((b, tc, heads * dv), BF16),
                   jax.ShapeDtypeStruct(s0.shape, F32)),
        grid=(b, nc),
        in_specs=[pl.BlockSpec((1, c, cdim), lambda bi, ci: (bi, ci, 0)),
                  pl.BlockSpec((1, c, heads * dv), lambda bi, ci: (bi, ci, 0)),
                  pl.BlockSpec((1, c, LANE), lambda bi, ci: (bi, ci, 0)),
                  pl.BlockSpec((LANE, c), lambda bi, ci: (0, bi * nc + ci)),
                  pl.BlockSpec((1, 8, cdim), lambda bi, ci: (bi, 0, 0)),
                  pl.BlockSpec((1, heads, dk, dv), lambda bi, ci: (bi, 0, 0, 0)),
                  _full(conv_w8.shape), _full(vec.shape), _full(cols.shape)],
        out_specs=(pl.BlockSpec((1, c, heads * dv), lambda bi, ci: (bi, ci, 0)),
                   pl.BlockSpec((1, heads, dk, dv), lambda bi, ci: (bi, 0, 0, 0))),
        scratch_shapes=[pltpu.VMEM((c + 8, cdim), F32)],
        compiler_params=_params("parallel", "arbitrary"),
        name="gdn_core",
    )(qkv, z, ba, bat, buf8, s0, conv_w8, vec, cols)


def _gdn_layer(xp, xs, bp, tp, bs, ts, mem_state, conv_state, g_mix,
               w_in, conv_w, a_log, dt_bias, norm_w, w_o):
    _, heads, dk, dv = mem_state.shape
    cdim = conv_state.shape[2]
    val = heads * dv
    kw = conv_w.shape[0]
    c = GDN_CHUNK
    d = w_in.shape[0]
    n_in = w_in.shape[1]
    w_pad = jnp.pad(w_in, ((0, 0), (0, cdim + val + LANE - n_in))).astype(BF16)
    w_ba_t = jnp.transpose(w_pad[:, cdim + val:])
    conv_w8 = jnp.pad(conv_w, ((0, 8 - kw), (0, 0)))
    lane_pad = lambda v, off: jnp.pad(v, (off, LANE - off - v.shape[0]))
    vec = jnp.stack([lane_pad(a_log, heads), lane_pad(dt_bias, heads), norm_w]
                    + [jnp.zeros((LANE,), F32)] * 5)
    cols = jnp.stack([jnp.broadcast_to(lane_pad(a_log, heads)[:, None], (LANE, c)),
                      jnp.broadcast_to(lane_pad(dt_bias, heads)[:, None], (LANE, c))])
    w_o_b = w_o.astype(BF16)

    def run(x, b, t, buf, s0):
        qkv, z, ba, bat = _gdn_proj(x, g_mix, w_pad, w_ba_t, cdim, val)
        qkv3 = qkv.reshape(b, t, cdim)
        xp_rows = jnp.concatenate([buf, qkv3], axis=1)
        new_buf = xp_rows[:, xp_rows.shape[1] - (kw - 1):]
        buf8 = jnp.pad(buf, ((0, 0), (8 - (kw - 1), 0), (0, 0)))
        if t % c == 0:
            valid = c
            z3, ba3 = z.reshape(b, t, val), ba.reshape(b, t, LANE)
        else:
            assert t < c
            valid = t
            padt = lambda a: jnp.pad(a, ((0, 0), (0, c - t), (0, 0)))
            qkv3, z3, ba3 = padt(qkv3), padt(z.reshape(b, t, val)), padt(ba.reshape(b, t, LANE))
            bat = jnp.pad(bat.reshape(LANE, b, t), ((0, 0), (0, 0), (0, c - t))).reshape(LANE, b * c)
        o, s_new = _gdn_core(qkv3, z3, ba3, bat, buf8, s0, conv_w8, vec, cols, valid, kw)
        o = o[:, :t].reshape(b * t, val)
        return _out_proj(x, o, w_o_b), new_buf, s_new

    yp, conv_p, mem_p = run(xp, bp, tp, jnp.zeros((bp, kw - 1, cdim), F32),
                            jnp.zeros((bp, heads, dk, dv), F32))
    ys, conv_s, mem_s = run(xs, bs, ts, conv_state, mem_state)
    return yp, ys, mem_p, conv_p, mem_s, conv_s


def kernel(x_prompt, x_sample, cache_mla_ckv, cache_mla_krope, cache_sb_k, cache_sb_v, state_gdn_mem, state_gdn_conv, page_table, norm_ffn, ffn_w_gate, ffn_w_up, ffn_w_down, norm_mix, norm_final, mla_w_in, mla_norm_q, mla_norm_kv, mla_w_uq, mla_w_uk, mla_w_uv, mla_w_o, sb_w_in, sb_w_o, gdn_w_in, gdn_conv_w, gdn_a_log, gdn_dt_bias, gdn_norm_w, gdn_w_o):
    bp, tp, d = x_prompt.shape
    bs, ts, _ = x_sample.shape
    depth = norm_mix.shape[0]
    xp = x_prompt.reshape(bp * tp, d)
    xs = x_sample.reshape(bs * ts, d)
    wg = ffn_w_gate.astype(BF16)
    wu = ffn_w_up.astype(BF16)
    wd = ffn_w_down.astype(BF16)
    outs = {k: [] for k in ("mla_p", "mla_s", "sb_p", "sb_s", "gdn_p", "gdn_s")}
    for layer in range(depth):
        kind, idx = layer % 3, layer // 3
        ffn0 = (norm_ffn[layer, 0], wg[layer, 0], wu[layer, 0], wd[layer, 0])
        ffn1 = (norm_ffn[layer, 1], wg[layer, 1], wu[layer, 1], wd[layer, 1])
        xp, xs = _ffn_half(xp, *ffn0), _ffn_half(xs, *ffn0)
        if kind == 0:
            xp, xs, a, b_, c_, d_ = _mla_layer(
                xp, xs, bp, tp, bs, ts, cache_mla_ckv, cache_mla_krope, idx, page_table, norm_mix[layer],
                mla_w_in[idx], mla_norm_q[idx], mla_norm_kv[idx], mla_w_uq[idx], mla_w_uk[idx],
                mla_w_uv[idx], mla_w_o[idx])
            outs["mla_p"].append((a, b_))
            outs["mla_s"].append((c_, d_))
        elif kind == 1:
            xp, xs, a, b_, c_, d_ = _sb_layer(
                xp, xs, bp, tp, bs, ts, cache_sb_k, cache_sb_v, idx, page_table, norm_mix[layer],
                sb_w_in[idx], sb_w_o[idx])
            outs["sb_p"].append((a, b_))
            outs["sb_s"].append((c_, d_))
        else:
            xp, xs, a, b_, c_, d_ = _gdn_layer(
                xp, xs, bp, tp, bs, ts, state_gdn_mem[idx], state_gdn_conv[idx], norm_mix[layer],
                gdn_w_in[idx], gdn_conv_w[idx], gdn_a_log[idx], gdn_dt_bias[idx], gdn_norm_w[idx],
                gdn_w_o[idx])
            outs["gdn_p"].append((a, b_))
            outs["gdn_s"].append((c_, d_))
        fin = norm_final if layer == depth - 1 else None
        xp, xs = _ffn_half(xp, *ffn1, final_g=fin), _ffn_half(xs, *ffn1, final_g=fin)

    def stacked(key, i):
        return jnp.stack([o[i] for o in outs[key]])

    return (xp.reshape(bp, tp, d), xs.reshape(bs, ts, d),
            stacked("mla_p", 0), stacked("mla_p", 1), stacked("sb_p", 0), stacked("sb_p", 1),
            stacked("gdn_p", 0), stacked("gdn_p", 1),
            stacked("mla_s", 0), stacked("mla_s", 1), stacked("sb_s", 0), stacked("sb_s", 1),
            stacked("gdn_s", 0), stacked("gdn_s", 1))
```
